```python
import jax, jax.numpy as jnp
from jax import lax
import numpy as np

D_MODEL = 1024
BATCH = 1
SEQ = 16384
DEPTH = 1
DEC_BATCH = 128
DEC_SEQ = 8
PAST_LEN = 8192
PAGE_SIZE = 128

HEAD_DIM = 64
MIX_WIDTH = D_MODEL
MOBA_HEADS = 8
SGU_GROUPS = 4
MEM_HEADS = 4
MOBA_W = MOBA_HEADS * HEAD_DIM
SGU_W = SGU_GROUPS * HEAD_DIM
MEM_W = MEM_HEADS * HEAD_DIM
IN_W = 3 * MOBA_W + 2 * SGU_W + MEM_W
MOBA_BLOCK = 256
MOBA_TOPK = 3
Q_BLOCK = 128
SGU_CHUNK = 128
N_MEM = 256
N_EXPERTS = 64
EXPERT_DIM = 256
TOP_K = 8
N_EXPERT_GROUPS = 8
TOPK_GROUPS = 4
ROUTE_SCALE = 2.5
SHARED_DIM = 256
TOKEN_BLOCK = 1024
LN_EPS = 1e-5
NEG = -1e30
DEEPNORM_ALPHA = (2 * DEPTH) ** 0.25
DEEPNORM_BETA = (8 * DEPTH) ** -0.25

kernel_name = 'hymba_moba_sgu_mem_moe_step'


def _layer_norm(x, g, b):
    xf = x.astype(jnp.float32)
    mu = jnp.mean(xf, axis=-1, keepdims=True)
    var = jnp.mean(jnp.square(xf - mu), axis=-1, keepdims=True)
    return ((xf - mu) * lax.rsqrt(var + LN_EPS) * g + b).astype(x.dtype)


def _alibi_slopes():
    h = jnp.arange(1, MOBA_HEADS + 1, dtype=jnp.float32)
    return jnp.exp2(-8.0 * h / MOBA_HEADS)


def _heads(z, n_heads):
    return z.reshape(*z.shape[:-1], n_heads, HEAD_DIM)


def _in_proj(x, w_in, sgu_ln_g, sgu_ln_b):
    splits = [MOBA_W, 2 * MOBA_W, 3 * MOBA_W, 3 * MOBA_W + SGU_W, 3 * MOBA_W + 2 * SGU_W]
    q, k, v, u, sv, qm = jnp.split(x @ w_in, splits, axis=-1)
    u = jax.nn.gelu(u)
    svn = _layer_norm(jax.nn.gelu(sv), sgu_ln_g, sgu_ln_b)
    return (_heads(q, MOBA_HEADS), _heads(k, MOBA_HEADS), _heads(v, MOBA_HEADS), u, svn, _heads(qm, MEM_HEADS))


def _to_pages(z):
    b, s, h, d = z.shape
    return z.reshape(b, s // PAGE_SIZE, PAGE_SIZE, h, d).transpose(0, 1, 3, 2, 4)


def _moba_prompt(q, k, v, slopes):
    b, s, h, d = q.shape
    n_blk = -(-s // MOBA_BLOCK)
    pad = n_blk * MOBA_BLOCK - s

    def to_blocks(z):
        z = jnp.pad(z, ((0, 0), (0, pad), (0, 0), (0, 0)))
        return z.reshape(b, n_blk, MOBA_BLOCK, h, d).transpose(0, 1, 3, 2, 4)

    kb, vb = to_blocks(k), to_blocks(v)
    k_mean = jnp.mean(kb.astype(jnp.float32), axis=3)
    n_sel = min(MOBA_TOPK, n_blk - 1)
    scale = HEAD_DIM ** -0.5
    n_qb = s // Q_BLOCK
    q_all = q.reshape(b, n_qb, Q_BLOCK, h, d).transpose(1, 0, 3, 2, 4)
    b_idx = jnp.arange(b)[:, None, None, None]
    h_idx = jnp.arange(h)[None, :, None, None]
    blk_pos = jnp.arange(MOBA_BLOCK)

    def block_step(args):
        qi, qblk = args
        t = qi * Q_BLOCK + jnp.arange(Q_BLOCK)
        own = (qi * Q_BLOCK) // MOBA_BLOCK
        k_own = lax.dynamic_index_in_dim(kb, own, axis=1, keepdims=False)
        v_own = lax.dynamic_index_in_dim(vb, own, axis=1, keepdims=False)
        dist_own = t[:, None] - (own * MOBA_BLOCK + blk_pos)[None, :]
        l_own = jnp.einsum('bhqd,bhsd->bhqs', qblk, k_own).astype(jnp.float32) * scale - slopes[:, None, None] * dist_own
        l_own = jnp.where(dist_own >= 0, l_own, NEG)
        if n_sel == 0:
            p = jax.nn.softmax(l_own, axis=-1)
            return jnp.einsum('bhqs,bhsd->bhqd', p.astype(v_own.dtype), v_own)
        gate = jnp.einsum('bhqd,bnhd->bhqn', qblk.astype(jnp.float32), k_mean)
        gate = jnp.where(jnp.arange(n_blk) < own, gate, -jnp.inf)
        _, sel = lax.top_k(gate, n_sel)
        k_sel = kb[b_idx, sel, h_idx]
        v_sel = vb[b_idx, sel, h_idx]
        dist_sel = t[:, None, None] - (sel[..., None] * MOBA_BLOCK + blk_pos)
        l_sel = jnp.einsum('bhqd,bhqnsd->bhqns', qblk, k_sel).astype(jnp.float32) * scale - slopes[:, None, None, None] * dist_sel
        l_sel = jnp.where((sel < own)[..., None], l_sel, NEG)
        n_g = n_sel * MOBA_BLOCK
        p = jax.nn.softmax(jnp.concatenate([l_sel.reshape(b, h, Q_BLOCK, n_g), l_own], axis=-1), axis=-1)
        p_sel = p[..., :n_g].reshape(b, h, Q_BLOCK, n_sel, MOBA_BLOCK).astype(v_sel.dtype)
        p_own = p[..., n_g:].astype(v_own.dtype)
        return jnp.einsum('bhqns,bhqnsd->bhqd', p_sel, v_sel) + jnp.einsum('bhqs,bhsd->bhqd', p_own, v_own)

    o = lax.map(block_step, (jnp.arange(n_qb), q_all))
    return o.transpose(1, 0, 3, 2, 4).reshape(b, s, h * d)


def _moba_sample(q, k_new, v_new, cache_k, cache_v, layer, page_table, slopes):
    db, t_new, h, d = q.shape
    ppb = MOBA_BLOCK // PAGE_SIZE
    n_fp = PAST_LEN // MOBA_BLOCK
    own_start = n_fp * MOBA_BLOCK
    scale = HEAD_DIM ** -0.5
    t = PAST_LEN + jnp.arange(t_new)
    blk_pos = jnp.arange(MOBA_BLOCK)
    own_pages = page_table[:, own_start // PAGE_SIZE:PAST_LEN // PAGE_SIZE]
    n_op = own_pages.shape[1]

    def own_rows(cache, new):
        past = cache[layer, own_pages].transpose(0, 2, 1, 3, 4).reshape(db, h, n_op * PAGE_SIZE, d)
        return jnp.concatenate([past, new.transpose(0, 2, 1, 3).astype(past.dtype)], axis=2)

    k_own, v_own = own_rows(cache_k, k_new), own_rows(cache_v, v_new)
    s_own = jnp.concatenate([jnp.arange(own_start, PAST_LEN), t])
    dist_own = t[:, None] - s_own[None, :]
    l_own = jnp.einsum('bthd,bhsd->bhts', q, k_own).astype(jnp.float32) * scale - slopes[:, None, None] * dist_own
    l_own = jnp.where(dist_own >= 0, l_own, NEG)
    n_sel = min(MOBA_TOPK, n_fp)
    if n_sel == 0:
        p = jax.nn.softmax(l_own, axis=-1)
        o = jnp.einsum('bhts,bhsd->bthd', p.astype(v_own.dtype), v_own)
        return o.reshape(db, t_new, h * d)
    fp_pages = page_table[:, :n_fp * ppb].T
    page_mean = lax.map(lambda pcol: jnp.mean(cache_k[layer, pcol].astype(jnp.float32), axis=2), fp_pages)
    k_mean = jnp.mean(page_mean.reshape(n_fp, ppb, db, h, d), axis=1)
    gate = jnp.einsum('bthd,nbhd->bhtn', q.astype(jnp.float32), k_mean)
    _, sel = lax.top_k(gate, n_sel)
    logical = sel[..., None] * ppb + jnp.arange(ppb)
    phys = page_table[jnp.arange(db)[:, None, None, None, None], logical]
    h_idx = jnp.arange(h)[None, :, None, None]
    n_g = n_sel * MOBA_BLOCK

    def tok_step(args):
        q_i, sel_i, phys_i, l_own_i, t_i = args
        k_sel = cache_k[layer, phys_i, h_idx].reshape(db, h, n_sel, MOBA_BLOCK, d)
        v_sel = cache_v[layer, phys_i, h_idx].reshape(db, h, n_sel, MOBA_BLOCK, d)
        dist_sel = t_i - (sel_i[..., None] * MOBA_BLOCK + blk_pos)
        l_sel = jnp.einsum('bhd,bhnsd->bhns', q_i, k_sel).astype(jnp.float32) * scale - slopes[:, None, None] * dist_sel
        p = jax.nn.softmax(jnp.concatenate([l_sel.reshape(db, h, n_g), l_own_i], axis=-1), axis=-1)
        p_sel = p[..., :n_g].reshape(db, h, n_sel, MOBA_BLOCK).astype(v_sel.dtype)
        p_own = p[..., n_g:].astype(v_own.dtype)
        return jnp.einsum('bhns,bhnsd->bhd', p_sel, v_sel) + jnp.einsum('bhs,bhsd->bhd', p_own, v_own)

    o = lax.map(tok_step, (q.transpose(1, 0, 2, 3), sel.transpose(2, 0, 1, 3), phys.transpose(2, 0, 1, 3, 4), l_own.transpose(2, 0, 1, 3), t))
    return o.transpose(1, 0, 2, 3).reshape(db, t_new, h * d)


def _sgu_prompt(u, vn, w_s, b_s):
    b, s, _ = u.shape
    ws = jnp.tril(w_s)
    vc = vn.reshape(b, s // SGU_CHUNK, SGU_CHUNK, SGU_GROUPS, HEAD_DIM)
    z = jnp.einsum('gts,bcsgd->bctgd', ws, vc) + b_s.T[:, :, None]
    return u * z.reshape(b, s, SGU_W)


def _sgu_sample(u, vn, w_s, b_s):
    db, t_new, _ = u.shape
    ws = jnp.tril(w_s)[:, :t_new, :t_new]
    vs = vn.reshape(db, t_new, SGU_GROUPS, HEAD_DIM)
    z = jnp.einsum('gts,bsgd->btgd', ws, vs) + b_s[:, :t_new].T[:, :, None]
    return u * z.reshape(db, t_new, SGU_W)


def _mem_kv(mem, w_mem_kv):
    mk, mv = jnp.split(mem @ w_mem_kv, 2, axis=-1)
    return _heads(mk, MEM_HEADS), _heads(mv, MEM_HEADS)


def _mem_attend(qm, mk, mv):
    logits = jnp.einsum('bshd,bmhd->bhsm', qm, mk).astype(jnp.float32) * (HEAD_DIM ** -0.5)
    p = jax.nn.softmax(logits, axis=-1).astype(mv.dtype)
    o = jnp.einsum('bhsm,bmhd->bshd', p, mv)
    return o.reshape(*o.shape[:2], MEM_W)


def _mix_residual(x, o_a, o_b, o_c, w_out, g, b):
    o = jnp.concatenate([o_a, o_b, o_c], axis=-1) @ w_out
    return _layer_norm(DEEPNORM_ALPHA * x + o, g, b)


def _route(xt, w_router, router_bias):
    n = xt.shape[0]
    scores = jax.nn.sigmoid((xt @ w_router).astype(jnp.float32))
    biased = scores + router_bias.astype(jnp.float32)
    grp = biased.reshape(n, N_EXPERT_GROUPS, N_EXPERTS // N_EXPERT_GROUPS)
    grp_score = jnp.sum(lax.top_k(grp, 2)[0], axis=-1)
    _, top_grp = lax.top_k(grp_score, TOPK_GROUPS)
    grp_mask = jnp.sum(jax.nn.one_hot(top_grp, N_EXPERT_GROUPS, dtype=jnp.float32), axis=-2) > 0
    e_mask = jnp.repeat(grp_mask, N_EXPERTS // N_EXPERT_GROUPS, axis=-1)
    _, top_e = lax.top_k(jnp.where(e_mask, biased, -jnp.inf), TOP_K)
    w = jnp.take_along_axis(scores, top_e, axis=-1)
    w = w / jnp.sum(w, axis=-1, keepdims=True) * ROUTE_SCALE
    return jnp.zeros_like(scores).at[jnp.arange(n)[:, None], top_e].set(w)


def _moe(x, w_router, router_bias, w_gate, w_up, w_down, ws_gate, ws_up, ws_down):
    lead = x.shape[:-1]
    xt = x.reshape(-1, D_MODEL)
    n = xt.shape[0]
    gates = _route(xt, w_router, router_bias)
    tb = min(TOKEN_BLOCK, n)
    n_blk = -(-n // tb)
    pad = n_blk * tb - n
    xb = jnp.pad(xt, ((0, pad), (0, 0))).reshape(n_blk, tb, D_MODEL)
    gb = jnp.pad(gates, ((0, pad), (0, 0))).reshape(n_blk, tb, N_EXPERTS)

    def expert_block(args):
        x_i, g_i = args
        hid = jax.nn.silu(jnp.einsum('nd,edf->nef', x_i, w_gate)) * jnp.einsum('nd,edf->nef', x_i, w_up)
        return jnp.einsum('nef,efd->nd', hid * g_i[..., None].astype(hid.dtype), w_down)

    routed = lax.map(expert_block, (xb, gb)).reshape(-1, D_MODEL)[:n]
    shared = (jax.nn.silu(xt @ ws_gate) * (xt @ ws_up)) @ ws_down
    return (routed + shared).reshape(*lead, D_MODEL)


def setup_inputs(seed: int = 0) -> dict:
    key = jax.random.key(seed)
    ks = jax.random.split(key, 32)
    f32 = jnp.float32

    def nrm(k, shape, s):
        return jax.random.normal(k, shape, f32) * s

    n_pages = PAST_LEN // PAGE_SIZE
    n_pool = (DEC_BATCH * n_pages * 5) // 4
    perm = jax.random.permutation(ks[0], n_pool)
    page_table = perm[:DEC_BATCH * n_pages].reshape(DEC_BATCH, n_pages).astype(jnp.int32)
    return {
        'x_prompt': nrm(ks[1], (BATCH, SEQ, D_MODEL), 1.0),
        'x_sample': nrm(ks[2], (DEC_BATCH, DEC_SEQ, D_MODEL), 1.0),
        'mem_prompt': nrm(ks[3], (BATCH, N_MEM, D_MODEL), 1.0),
        'cache_k': nrm(ks[4], (DEPTH, n_pool, MOBA_HEADS, PAGE_SIZE, HEAD_DIM), 1.0),
        'cache_v': nrm(ks[5], (DEPTH, n_pool, MOBA_HEADS, PAGE_SIZE, HEAD_DIM), 1.0),
        'page_table': page_table,
        'cache_mem_k': nrm(ks[6], (DEPTH, DEC_BATCH, N_MEM, MEM_HEADS, HEAD_DIM), 1.0),
        'cache_mem_v': nrm(ks[7], (DEPTH, DEC_BATCH, N_MEM, MEM_HEADS, HEAD_DIM), 1.0),
        'w_in': nrm(ks[8], (DEPTH, D_MODEL, IN_W), D_MODEL ** -0.5),
        'w_out': nrm(ks[9], (DEPTH, MIX_WIDTH, D_MODEL), MIX_WIDTH ** -0.5 * DEEPNORM_BETA),
        'w_mem_kv': nrm(ks[10], (DEPTH, D_MODEL, 2 * MEM_W), D_MODEL ** -0.5),
        'sgu_ln_g': 1.0 + nrm(ks[11], (DEPTH, SGU_W), 0.01),
        'sgu_ln_b': nrm(ks[12], (DEPTH, SGU_W), 0.01),
        'sgu_w_s': nrm(ks[13], (DEPTH, SGU_GROUPS, SGU_CHUNK, SGU_CHUNK), SGU_CHUNK ** -0.5),
        'sgu_b_s': 1.0 + nrm(ks[14], (DEPTH, SGU_GROUPS, SGU_CHUNK), 0.01),
        'ln1_g': 1.0 + nrm(ks[15], (DEPTH, D_MODEL), 0.01),
        'ln1_b': nrm(ks[16], (DEPTH, D_MODEL), 0.01),
        'w_router': nrm(ks[17], (DEPTH, D_MODEL, N_EXPERTS), D_MODEL ** -0.5),
        'router_bias': nrm(ks[18], (DEPTH, N_EXPERTS), 0.01),
        'w_gate': nrm(ks[19], (DEPTH, N_EXPERTS, D_MODEL, EXPERT_DIM), D_MODEL ** -0.5),
        'w_up': nrm(ks[20], (DEPTH, N_EXPERTS, D_MODEL, EXPERT_DIM), D_MODEL ** -0.5),
        'w_down': nrm(ks[21], (DEPTH, N_EXPERTS, EXPERT_DIM, D_MODEL), EXPERT_DIM ** -0.5 * DEEPNORM_BETA),
        'ws_gate': nrm(ks[22], (DEPTH, D_MODEL, SHARED_DIM), D_MODEL ** -0.5),
        'ws_up': nrm(ks[23], (DEPTH, D_MODEL, SHARED_DIM), D_MODEL ** -0.5),
        'ws_down': nrm(ks[24], (DEPTH, SHARED_DIM, D_MODEL), SHARED_DIM ** -0.5 * DEEPNORM_BETA),
        'ln2_g': 1.0 + nrm(ks[25], (DEPTH, D_MODEL), 0.01),
        'ln2_b': nrm(ks[26], (DEPTH, D_MODEL), 0.01),
    }


def reference(x_prompt, x_sample, mem_prompt, cache_k, cache_v, page_table, cache_mem_k, cache_mem_v,
              w_in, w_out, w_mem_kv, sgu_ln_g, sgu_ln_b, sgu_w_s, sgu_b_s, ln1_g, ln1_b,
              w_router, router_bias, w_gate, w_up, w_down, ws_gate, ws_up, ws_down, ln2_g, ln2_b):
    slopes = _alibi_slopes()
    xp, xs = x_prompt, x_sample
    k_p, v_p, k_s, v_s, mk_p, mv_p, sv_s = [], [], [], [], [], [], []
    for l in range(DEPTH):
        moe_args = (w_router[l], router_bias[l], w_gate[l], w_up[l], w_down[l], ws_gate[l], ws_up[l], ws_down[l])
        q, k, v, u, svn, qm = _in_proj(xp, w_in[l], sgu_ln_g[l], sgu_ln_b[l])
        mk, mv = _mem_kv(mem_prompt, w_mem_kv[l])
        h1 = _mix_residual(xp, _moba_prompt(q, k, v, slopes), _sgu_prompt(u, svn, sgu_w_s[l], sgu_b_s[l]),
                           _mem_attend(qm, mk, mv), w_out[l], ln1_g[l], ln1_b[l])
        xp = _layer_norm(DEEPNORM_ALPHA * h1 + _moe(h1, *moe_args), ln2_g[l], ln2_b[l])
        k_p.append(_to_pages(k))
        v_p.append(_to_pages(v))
        mk_p.append(mk)
        mv_p.append(mv)
        q, k, v, u, svn, qm = _in_proj(xs, w_in[l], sgu_ln_g[l], sgu_ln_b[l])
        o_a = _moba_sample(q, k, v, cache_k, cache_v, l, page_table, slopes)
        h1 = _mix_residual(xs, o_a, _sgu_sample(u, svn, sgu_w_s[l], sgu_b_s[l]),
                           _mem_attend(qm, cache_mem_k[l], cache_mem_v[l]), w_out[l], ln1_g[l], ln1_b[l])
        xs = _layer_norm(DEEPNORM_ALPHA * h1 + _moe(h1, *moe_args), ln2_g[l], ln2_b[l])
        k_s.append(k.transpose(0, 2, 1, 3))
        v_s.append(v.transpose(0, 2, 1, 3))
        sv_s.append(svn)
    return (xp, xs, jnp.stack(k_p), jnp.stack(v_p), jnp.stack(k_s), jnp.stack(v_s), jnp.stack(mk_p), jnp.stack(mv_p), jnp.stack(sv_s))
```

```python
import functools

import jax
import jax.numpy as jnp
from jax import lax
from jax.experimental import pallas as pl
from jax.experimental.pallas import tpu as pltpu

F32 = jnp.float32
BF16 = jnp.bfloat16

HEAD_DIM = 64
MOBA_HEADS = 8
SGU_GROUPS = 4
MEM_HEADS = 4
MOBA_W = MOBA_HEADS * HEAD_DIM
SGU_W = SGU_GROUPS * HEAD_DIM
MEM_W = MEM_HEADS * HEAD_DIM
MOBA_BLOCK = 256
MOBA_TOPK = 3
Q_BLOCK = 128
SGU_CHUNK = 128
PAGE_SIZE = 128
N_EXPERT_GROUPS = 8
TOPK_GROUPS = 4
TOP_K = 8
ROUTE_SCALE = 2.5
LN_EPS = 1e-5
NEG = -1e30
DEPTH = 1
DEEPNORM_ALPHA = (2 * DEPTH) ** 0.25
SCALE = HEAD_DIM ** -0.5
LANES = 128
VMEM_LIMIT = 48 * 1024 * 1024
HIGHEST = lax.Precision.HIGHEST


def _cparams(sem):
    return pltpu.CompilerParams(dimension_semantics=sem, vmem_limit_bytes=VMEM_LIMIT)


def _gelu(x):
    return x * (0.5 * (1.0 + jnp.tanh(0.7978845608028654 * (x + 0.044715 * (x * x * x)))))


def _sigmoid(x):
    return 1.0 / (1.0 + jnp.exp(-x))


def _silu(x):
    return x * _sigmoid(x)


def _layer_norm(x, g, b):
    mu = jnp.mean(x, axis=-1, keepdims=True)
    xc = x - mu
    var = jnp.mean(xc * xc, axis=-1, keepdims=True)
    return xc * lax.rsqrt(var + LN_EPS) * g + b


def _dot(a, b):
    return jnp.dot(a, b, preferred_element_type=F32)


def _dot_nt(a, b, precision=None):
    return lax.dot_general(a, b, (((1,), (1,)), ((), ())), precision=precision, preferred_element_type=F32)


def _inproj_body(x_ref, w_ref, g_ref, b_ref, q_ref, k_ref, v_ref, u_ref, sv_ref, qm_ref):
    x = x_ref[...].astype(BF16)
    c0, c1, c2, c3, c4 = MOBA_W, 2 * MOBA_W, 3 * MOBA_W, 3 * MOBA_W + SGU_W, 3 * MOBA_W + 2 * SGU_W
    q_ref[...] = _dot(x, w_ref[:, 0:c0])
    k_ref[...] = _dot(x, w_ref[:, c0:c1])
    v_ref[...] = _dot(x, w_ref[:, c1:c2])
    u_ref[...] = _gelu(_dot(x, w_ref[:, c2:c3]))
    sv_ref[...] = _layer_norm(_gelu(_dot(x, w_ref[:, c3:c4])), g_ref[...], b_ref[...])
    qm_ref[...] = _dot(x, w_ref[:, c4:c4 + MEM_W])


def _in_proj(x, w_in_bf, sgu_g, sgu_b):
    n, d = x.shape
    tm = min(512, n)
    in_w = w_in_bf.shape[1]
    row = lambda w: pl.BlockSpec((tm, w), lambda i: (i, 0))
    full = lambda a: pl.BlockSpec(a.shape, lambda i: (0,) * a.ndim)
    widths = (MOBA_W, MOBA_W, MOBA_W, SGU_W, SGU_W, MEM_W)
    return pl.pallas_call(
        _inproj_body,
        grid=(n // tm,),
        in_specs=[row(d), full(w_in_bf), full(sgu_g), full(sgu_b)],
        out_specs=[row(w) for w in widths],
        out_shape=[jax.ShapeDtypeStruct((n, w), F32) for w in widths],
        compiler_params=_cparams(("parallel",)),
        name="in_proj",
    )(x, w_in_bf, sgu_g, sgu_b)


def _memkv_body(m_ref, w_ref, o_ref):
    o_ref[...] = _dot(m_ref[...].astype(BF16), w_ref[...])


def _mem_kv(mem, w_bf):
    n, _ = mem.shape
    return pl.pallas_call(
        _memkv_body,
        out_shape=jax.ShapeDtypeStruct((n, w_bf.shape[1]), F32),
        compiler_params=pltpu.CompilerParams(vmem_limit_bytes=VMEM_LIMIT),
        name="mem_kv",
    )(mem, w_bf)


def _kmean_body(k_ref, o_ref):
    kb = k_ref[...].reshape(8, MOBA_BLOCK, k_ref.shape[1])
    o_ref[...] = jnp.sum(kb, axis=1) * (1.0 / MOBA_BLOCK)


def _k_block_mean(k):
    s, w = k.shape
    nblk = s // MOBA_BLOCK
    return pl.pallas_call(
        _kmean_body,
        grid=(nblk // 8,),
        in_specs=[pl.BlockSpec((8 * MOBA_BLOCK, w), lambda i: (i, 0))],
        out_specs=pl.BlockSpec((8, w), lambda i: (i, 0)),
        out_shape=jax.ShapeDtypeStruct((nblk, w), F32),
        compiler_params=_cparams(("parallel",)),
        name="k_block_mean",
    )(k)


def _moba_prompt_body(qT_ref, k_ref, vT_ref, km_ref, oT_ref, bias_s, dmat_s, *, nblk):
    p = pl.program_id(0)
    qi = pl.program_id(1)
    own = qi // (MOBA_BLOCK // Q_BLOCK)
    half = qi % (MOBA_BLOCK // Q_BLOCK)
    w2 = 2 * Q_BLOCK

    lane = lax.broadcasted_iota(jnp.int32, (1, w2), 1)
    head = 2 * p + (lane >= Q_BLOCK).astype(jnp.int32)
    slope = jnp.exp2(-(head + 1).astype(F32))
    tl = lane % Q_BLOCK

    @pl.when(qi == 0)
    def _():
        sl = lax.broadcasted_iota(jnp.int32, (MOBA_BLOCK, w2), 0)
        dmat_s[...] = slope * (sl - tl).astype(F32)

    qT = qT_ref[...]
    z = jnp.zeros((HEAD_DIM, Q_BLOCK), F32)
    qbd = jnp.concatenate(
        [jnp.concatenate([qT[:HEAD_DIM], z], axis=1), jnp.concatenate([z, qT[HEAD_DIM:]], axis=1)], axis=0)
    qbd_s = (qbd * SCALE).astype(BF16)

    gate = jnp.dot(km_ref[...], qbd, precision=HIGHEST, preferred_element_type=F32)
    n_iota = lax.broadcasted_iota(jnp.int32, (nblk, w2), 0)
    gate = jnp.where(n_iota < own, gate, -jnp.inf)
    sel = jnp.logical_and(_rank_rows(gate, nblk, 0) < float(MOBA_TOPK), n_iota < own)
    blk_off = (n_iota * MOBA_BLOCK - qi * Q_BLOCK).astype(F32)
    bias_s[...] = jnp.where(sel, slope * blk_off, NEG)

    def scores(j):
        return _dot(k_ref[j], qbd_s) + dmat_s[...]

    sl = lax.broadcasted_iota(jnp.int32, (MOBA_BLOCK, w2), 0)
    dist = half * Q_BLOCK + tl - sl
    s0 = scores(own) - slope * (half * Q_BLOCK).astype(F32)
    s0 = jnp.where(dist >= 0, s0, NEG)
    m0 = jnp.max(s0, axis=0, keepdims=True)
    p0 = jnp.exp(s0 - m0)
    l0 = jnp.sum(p0, axis=0, keepdims=True)
    acc0 = _dot(vT_ref[own], p0.astype(BF16))

    def body(j, carry):
        m, l, acc = carry
        s = scores(j) + bias_s[pl.ds(j, 1), :]
        m_new = jnp.maximum(m, jnp.max(s, axis=0, keepdims=True))
        a = jnp.exp(m - m_new)
        pj = jnp.exp(s - m_new)
        l = a * l + jnp.sum(pj, axis=0, keepdims=True)
        acc = a * acc + _dot(vT_ref[j], pj.astype(BF16))
        return m_new, l, acc

    _, l, acc = lax.fori_loop(0, own, body, (m0, l0, acc0))
    inv = 1.0 / l
    oT_ref[0:HEAD_DIM, :] = acc[0:HEAD_DIM, 0:Q_BLOCK] * inv[:, 0:Q_BLOCK]
    oT_ref[HEAD_DIM:, :] = acc[HEAD_DIM:, Q_BLOCK:] * inv[:, Q_BLOCK:]


def _moba_prompt(qT, k3, vT3, kmean):
    w, s = qT.shape
    nblk = k3.shape[0]
    pw = 2 * HEAD_DIM
    return pl.pallas_call(
        functools.partial(_moba_prompt_body, nblk=nblk),
        grid=(w // pw, s // Q_BLOCK),
        in_specs=[
            pl.BlockSpec((pw, Q_BLOCK), lambda p, i: (p, i)),
            pl.BlockSpec((nblk, MOBA_BLOCK, pw), lambda p, i: (0, 0, p)),
            pl.BlockSpec((nblk, pw, MOBA_BLOCK), lambda p, i: (0, p, 0)),
            pl.BlockSpec((nblk, pw), lambda p, i: (0, p)),
        ],
        out_specs=pl.BlockSpec((pw, Q_BLOCK), lambda p, i: (p, i)),
        out_shape=jax.ShapeDtypeStruct((w, s), F32),
        scratch_shapes=[pltpu.VMEM((nblk, 2 * Q_BLOCK), F32), pltpu.VMEM((MOBA_BLOCK, 2 * Q_BLOCK), F32)],
        compiler_params=_cparams(("arbitrary", "arbitrary")),
        name="moba_prompt",
    )(qT, k3, vT3, kmean)


def _softmax_rows(logits):
    m = jnp.max(logits, axis=-1, keepdims=True)
    e = jnp.exp(logits - m)
    return e / jnp.sum(e, axis=-1, keepdims=True)


def _mem_prompt_body(qm_ref, mkv_ref, o_ref):
    outs = []
    for h in range(MEM_HEADS):
        lo, hi = h * HEAD_DIM, (h + 1) * HEAD_DIM
        q = (qm_ref[:, lo:hi] * SCALE).astype(BF16)
        mk = mkv_ref[:, lo:hi].astype(BF16)
        mv = mkv_ref[:, MEM_W + lo:MEM_W + hi].astype(BF16)
        p = _softmax_rows(_dot_nt(q, mk))
        outs.append(_dot(p.astype(BF16), mv))
    o_ref[...] = jnp.concatenate(outs, axis=1)


def _mem_attend_prompt(qm, mkv):
    n, w = qm.shape
    tm = min(512, n)
    return pl.pallas_call(
        _mem_prompt_body,
        grid=(n // tm,),
        in_specs=[pl.BlockSpec((tm, w), lambda i: (i, 0)), pl.BlockSpec(mkv.shape, lambda i: (0, 0))],
        out_specs=pl.BlockSpec((tm, w), lambda i: (i, 0)),
        out_shape=jax.ShapeDtypeStruct((n, w), F32),
        compiler_params=_cparams(("parallel",)),
        name="mem_attend_prompt",
    )(qm, mkv)


def _mem_sample_body(qm_ref, mkT_ref, mvT_ref, o_ref, *, sb):
    for i in range(sb):
        outs = []
        for h in range(MEM_HEADS):
            lo, hi = h * HEAD_DIM, (h + 1) * HEAD_DIM
            q = (qm_ref[i, :, lo:hi] * SCALE).astype(BF16)
            p = _softmax_rows(_dot(q, mkT_ref[i, h].astype(BF16)))
            outs.append(_dot_nt(p.astype(BF16), mvT_ref[i, h].astype(BF16)))
        o_ref[i] = jnp.concatenate(outs, axis=1)


def _mem_attend_sample(qm3, mkT, mvT):
    b, t, w = qm3.shape
    sb = min(8, b)
    blk = (sb,) + mkT.shape[1:]
    return pl.pallas_call(
        functools.partial(_mem_sample_body, sb=sb),
        grid=(b // sb,),
        in_specs=[pl.BlockSpec((sb, t, w), lambda i: (i, 0, 0)),
                  pl.BlockSpec(blk, lambda i: (i, 0, 0, 0)), pl.BlockSpec(blk, lambda i: (i, 0, 0, 0))],
        out_specs=pl.BlockSpec((sb, t, w), lambda i: (i, 0, 0)),
        out_shape=jax.ShapeDtypeStruct((b, t, w), F32),
        compiler_params=_cparams(("parallel",)),
        name="mem_attend_sample",
    )(qm3, mkT, mvT)


def _mix_body(x_ref, oa_ref, u_ref, vn_ref, oc_ref, wm_ref, bz_ref, wo_ref, g_ref, b_ref, h_ref, *, tm):
    lane_grp = lax.broadcasted_iota(jnp.int32, (SGU_CHUNK, SGU_W), 1) // HEAD_DIM
    zs = []
    for c in range(tm // SGU_CHUNK):
        vn = vn_ref[c * SGU_CHUNK:(c + 1) * SGU_CHUNK, :].astype(BF16)
        z = bz_ref[...]
        for g in range(SGU_GROUPS):
            z = z + jnp.where(lane_grp == g, _dot(wm_ref[g], vn), 0.0)
        zs.append(z)
    ob = u_ref[...] * jnp.concatenate(zs, axis=0)
    y = _dot(oa_ref[...].astype(BF16), wo_ref[0:MOBA_W, :])
    y = y + _dot(ob.astype(BF16), wo_ref[MOBA_W:MOBA_W + SGU_W, :])
    y = y + _dot(oc_ref[...].astype(BF16), wo_ref[MOBA_W + SGU_W:, :])
    h_ref[...] = _layer_norm(DEEPNORM_ALPHA * x_ref[...] + y, g_ref[...], b_ref[...])


def _mix(x, oa, u, vn, oc, wmat_bf, bz, wo_bf, g, b):
    n, d = x.shape
    tm = min(512, n)
    row = lambda a: pl.BlockSpec((tm, a.shape[1]), lambda i: (i, 0))
    full = lambda a: pl.BlockSpec(a.shape, lambda i: (0,) * a.ndim)
    return pl.pallas_call(
        functools.partial(_mix_body, tm=tm),
        grid=(n // tm,),
        in_specs=[row(x), row(oa), row(u), row(vn), row(oc), full(wmat_bf), full(bz), full(wo_bf), full(g), full(b)],
        out_specs=row(x),
        out_shape=jax.ShapeDtypeStruct((n, d), F32),
        compiler_params=_cparams(("parallel",)),
        name="mix",
    )(x, oa, u, vn, oc, wmat_bf, bz, wo_bf, g, b)


def _rank_rows(x, n_rows, axis):
    idx = lax.broadcasted_iota(jnp.int32, x.shape, axis)
    rank = jnp.zeros(x.shape, F32)
    for j in range(n_rows):
        xj = lax.slice_in_dim(x, j, j + 1, axis=axis)
        tie = jnp.where(idx > j, 1.0, 0.0)
        rank = rank + jnp.where(xj > x, 1.0, jnp.where(xj == x, tie, 0.0))
    return rank


def _router_body(h_ref, wrT_ref, rb_ref, o_ref, *, n_exp):
    tm = h_ref.shape[0]
    gsz = n_exp // N_EXPERT_GROUPS
    scores = _sigmoid(_dot_nt(wrT_ref[...], h_ref[...], precision=HIGHEST))
    biased = scores + rb_ref[...]
    b3 = biased.reshape(N_EXPERT_GROUPS, gsz, tm)
    top2 = jnp.sum(jnp.where(_rank_rows(b3, gsz, 1) < 2.0, b3, 0.0), axis=1, keepdims=True)
    gs = jnp.broadcast_to(top2, b3.shape).reshape(n_exp, tm)
    grp = lax.broadcasted_iota(jnp.int32, (n_exp, tm), 0) // gsz
    grank = jnp.zeros((n_exp, tm), F32)
    for g2 in range(N_EXPERT_GROUPS):
        xg = gs[g2 * gsz:g2 * gsz + 1, :]
        tie = jnp.where(grp > g2, 1.0, 0.0)
        grank = grank + jnp.where(xg > gs, 1.0, jnp.where(xg == gs, tie, 0.0))
    masked = jnp.where(grank < float(TOPK_GROUPS), biased, -jnp.inf)
    sel = _rank_rows(masked, n_exp, 0) < float(TOP_K)
    w = jnp.where(sel, scores, 0.0)
    o_ref[...] = w / jnp.sum(w, axis=0, keepdims=True) * ROUTE_SCALE


def _router(h, wrT, rb_col):
    n, d = h.shape
    n_exp = wrT.shape[0]
    tm = min(512, n)
    return pl.pallas_call(
        functools.partial(_router_body, n_exp=n_exp),
        grid=(n // tm,),
        in_specs=[pl.BlockSpec((tm, d), lambda i: (i, 0)), pl.BlockSpec(wrT.shape, lambda i: (0, 0)),
                  pl.BlockSpec(rb_col.shape, lambda i: (0, 0))],
        out_specs=pl.BlockSpec((n_exp, tm), lambda i: (0, i)),
        out_shape=jax.ShapeDtypeStruct((n_exp, n), F32),
        compiler_params=_cparams(("parallel",)),
        name="router",
    )(h, wrT, rb_col)


def _moe_body(h_ref, gt_ref, wg_ref, wu_ref, wd_ref, sg_ref, su_ref, sd_ref, g_ref, b_ref, o_ref, acc_ref, *, n_exp):
    e = pl.program_id(1)
    hb = h_ref[...].astype(BF16)

    @pl.when(e == 0)
    def _():
        hs = _silu(_dot(hb, sg_ref[...])) * _dot(hb, su_ref[...])
        acc_ref[...] = _dot(hs.astype(BF16), sd_ref[...])

    lane = lax.broadcasted_iota(jnp.int32, gt_ref.shape, 1)
    gcol = jnp.sum(jnp.where(lane == e, gt_ref[...], 0.0), axis=1, keepdims=True)
    hid = _silu(_dot(hb, wg_ref[0])) * _dot(hb, wu_ref[0]) * gcol
    acc_ref[...] += _dot(hid.astype(BF16), wd_ref[0])

    @pl.when(e == n_exp - 1)
    def _():
        o_ref[...] = _layer_norm(DEEPNORM_ALPHA * h_ref[...] + acc_ref[...], g_ref[...], b_ref[...])


def _moe(h, gates, wg, wu, wd, sg, su, sd, g, b):
    n, d = h.shape
    n_exp, _, f = wg.shape
    tm = min(1024, n)
    full = lambda a: pl.BlockSpec(a.shape, lambda i, e: (0,) * a.ndim)
    return pl.pallas_call(
        functools.partial(_moe_body, n_exp=n_exp),
        grid=(n // tm, n_exp),
        in_specs=[pl.BlockSpec((tm, d), lambda i, e: (i, 0)), pl.BlockSpec((tm, n_exp), lambda i, e: (i, 0)),
                  pl.BlockSpec((1, d, f), lambda i, e: (e, 0, 0)), pl.BlockSpec((1, d, f), lambda i, e: (e, 0, 0)),
                  pl.BlockSpec((1, f, d), lambda i, e: (e, 0, 0)),
                  full(sg), full(su), full(sd), full(g), full(b)],
        out_specs=pl.BlockSpec((tm, d), lambda i, e: (i, 0)),
        out_shape=jax.ShapeDtypeStruct((n, d), F32),
        scratch_shapes=[pltpu.VMEM((tm, d), F32)],
        compiler_params=_cparams(("parallel", "arbitrary")),
        name="moe",
    )(h, gates, wg, wu, wd, sg, su, sd, g, b)


def _sample_keys_body(pt_ref, q_ref, *refs, ppstep):
    pages, (sc_ref, km_ref) = refs[:ppstep], refs[ppstep:]
    s = pl.program_id(1)

    @pl.when(s == 0)
    def _():
        km_ref[...] = jnp.zeros(km_ref.shape, F32)

    lane = lax.broadcasted_iota(jnp.int32, (HEAD_DIM, LANES), 1)
    for i in range(ppstep):
        blk = (s * ppstep + i) // (MOBA_BLOCK // PAGE_SIZE)
        for h in range(MOBA_HEADS):
            kt = pages[i][0, h]
            rs = jnp.sum(kt, axis=1, keepdims=True)
            km_ref[0, h] += jnp.where(lane == blk, rs, 0.0)
            sc_ref[0, h, :, i * PAGE_SIZE:(i + 1) * PAGE_SIZE] = _dot(q_ref[0, h], kt.astype(BF16))


def _page_spec(n_pages, ppstep, i):
    return pl.BlockSpec((1, MOBA_HEADS, HEAD_DIM, PAGE_SIZE),
                        lambda b, s, pt: (pt[b * n_pages + s * ppstep + i], 0, 0, 0))


def _sample_keys(pt_flat, q4_bf, ckT, n_pages):
    b, _, t, _ = q4_bf.shape
    ppstep = min(8, n_pages)
    grid_spec = pltpu.PrefetchScalarGridSpec(
        num_scalar_prefetch=1,
        grid=(b, n_pages // ppstep),
        in_specs=[pl.BlockSpec((1, MOBA_HEADS, t, HEAD_DIM), lambda b_, s, pt: (b_, 0, 0, 0))]
        + [_page_spec(n_pages, ppstep, i) for i in range(ppstep)],
        out_specs=[pl.BlockSpec((1, MOBA_HEADS, t, ppstep * PAGE_SIZE), lambda b_, s, pt: (b_, 0, 0, s)),
                   pl.BlockSpec((1, MOBA_HEADS, HEAD_DIM, LANES), lambda b_, s, pt: (b_, 0, 0, 0))],
    )
    return pl.pallas_call(
        functools.partial(_sample_keys_body, ppstep=ppstep),
        grid_spec=grid_spec,
        out_shape=[jax.ShapeDtypeStruct((b, MOBA_HEADS, t, n_pages * PAGE_SIZE), F32),
                   jax.ShapeDtypeStruct((b, MOBA_HEADS, HEAD_DIM, LANES), F32)],
        compiler_params=_cparams(("arbitrary", "arbitrary")),
        name="moba_sample_keys",
    )(pt_flat, q4_bf, *([ckT] * ppstep))


def _sample_values_body(pt_ref, sc_ref, km_ref, q_ref, kn_ref, vn_ref, *refs, ppstep, n_pages, t_new):
    pages, o_ref, (p_s, l_s, acc_s) = refs[:ppstep], refs[ppstep], refs[ppstep + 1:]
    s = pl.program_id(1)
    ppb = MOBA_BLOCK // PAGE_SIZE
    n_fp = n_pages // ppb
    n_sel = min(MOBA_TOPK, n_fp)
    past = n_pages * PAGE_SIZE

    @pl.when(s == 0)
    def _():
        blane = lax.broadcasted_iota(jnp.int32, (t_new, LANES), 1)
        t_i = lax.broadcasted_iota(jnp.int32, (t_new, MOBA_BLOCK), 0)
        pos = lax.broadcasted_iota(jnp.int32, (t_new, MOBA_BLOCK), 1)
        to = lax.broadcasted_iota(jnp.int32, (t_new, t_new), 0)
        so = lax.broadcasted_iota(jnp.int32, (t_new, t_new), 1)
        for h in range(MOBA_HEADS):
            slope = 2.0 ** -(h + 1)
            qh = q_ref[0, h]
            gate = jnp.dot(qh, km_ref[0, h], precision=HIGHEST, preferred_element_type=F32) * (1.0 / MOBA_BLOCK)
            gate = jnp.where(blane < n_fp, gate, -jnp.inf)
            sel = jnp.where(_rank_rows(gate, n_fp, 1) < float(n_sel), 1.0, 0.0)
            qs = (qh * SCALE).astype(BF16)
            l_own = _dot_nt(qs, kn_ref[0, h].astype(BF16)) - slope * (to - so).astype(F32)
            l_own = jnp.where(to >= so, l_own, NEG)
            m = jnp.max(l_own, axis=1, keepdims=True)
            logits = []
            for n in range(n_fp):
                dist = (past - n * MOBA_BLOCK) + t_i - pos
                ln = sc_ref[0, h, :, n * MOBA_BLOCK:(n + 1) * MOBA_BLOCK] - slope * dist.astype(F32)
                ln = jnp.where(sel[:, n:n + 1] > 0.5, ln, NEG)
                m = jnp.maximum(m, jnp.max(ln, axis=1, keepdims=True))
                logits.append(ln)
            p_own = jnp.exp(l_own - m)
            lsum = jnp.sum(p_own, axis=1, keepdims=True)
            for n in range(n_fp):
                pn = jnp.exp(logits[n] - m)
                lsum = lsum + jnp.sum(pn, axis=1, keepdims=True)
                for c in range(ppb):
                    p_s[h, n * ppb + c] = pn[:, c * PAGE_SIZE:(c + 1) * PAGE_SIZE]
            l_s[h] = jnp.broadcast_to(lsum, (t_new, HEAD_DIM))
            acc_s[h] = _dot(p_own.astype(BF16), vn_ref[0, h].astype(BF16))

    for i in range(ppstep):
        pg = s * ppstep + i
        for h in range(MOBA_HEADS):
            acc_s[h] += _dot_nt(p_s[h, pg].astype(BF16), pages[i][0, h].astype(BF16))

    @pl.when(s == pl.num_programs(1) - 1)
    def _():
        o_ref[0] = acc_s[...] / l_s[...]


def _sample_values(pt_flat, scores, kmT, q4, kn4, vn4, cvT, n_pages):
    b, _, t, _ = q4.shape
    ppstep = min(8, n_pages)
    small = lambda a: pl.BlockSpec((1,) + a.shape[1:], lambda b_, s, pt: (b_,) + (0,) * (a.ndim - 1))
    grid_spec = pltpu.PrefetchScalarGridSpec(
        num_scalar_prefetch=1,
        grid=(b, n_pages // ppstep),
        in_specs=[small(scores), small(kmT), small(q4), small(kn4), small(vn4)]
        + [_page_spec(n_pages, ppstep, i) for i in range(ppstep)],
        out_specs=pl.BlockSpec((1, MOBA_HEADS, t, HEAD_DIM), lambda b_, s, pt: (b_, 0, 0, 0)),
        scratch_shapes=[pltpu.VMEM((MOBA_HEADS, n_pages, t, PAGE_SIZE), F32),
                        pltpu.VMEM((MOBA_HEADS, t, HEAD_DIM), F32),
                        pltpu.VMEM((MOBA_HEADS, t, HEAD_DIM), F32)],
    )
    return pl.pallas_call(
        functools.partial(_sample_values_body, ppstep=ppstep, n_pages=n_pages, t_new=t),
        grid_spec=grid_spec,
        out_shape=jax.ShapeDtypeStruct((b, MOBA_HEADS, t, HEAD_DIM), F32),
        compiler_params=_cparams(("arbitrary", "arbitrary")),
        name="moba_sample_values",
    )(pt_flat, scores, kmT, q4, kn4, vn4, *([cvT] * ppstep))


def _sgu_block_diag(w_s, t_new):
    g = w_s.shape[0]
    reps = SGU_CHUNK // t_new
    small = jnp.tril(w_s)[:, :t_new, :t_new]
    eye = jnp.eye(reps, dtype=w_s.dtype)
    return jnp.einsum("rq,gts->grtqs", eye, small).reshape(g, SGU_CHUNK, SGU_CHUNK)


def _sgu_bias(b_s, rows):
    return jnp.repeat(b_s[:, :rows].T, HEAD_DIM, axis=1)


def _ffn(h1, wrT, rb_col, moe_w, g, b):
    gates = _router(h1, wrT, rb_col).T
    return _moe(h1, gates, *moe_w, g, b)


def kernel(x_prompt, x_sample, mem_prompt, cache_k, cache_v, page_table, cache_mem_k, cache_mem_v, w_in, w_out, w_mem_kv, sgu_ln_g, sgu_ln_b, sgu_w_s, sgu_b_s, ln1_g, ln1_b, w_router, router_bias, w_gate, w_up, w_down, ws_gate, ws_up, ws_down, ln2_g, ln2_b):
    assert w_in.shape[0] == DEPTH and x_prompt.shape[0] == 1
    _, seq, d_model = x_prompt.shape
    db, t_new, _ = x_sample.shape
    n_pages = page_table.shape[1]
    assert n_pages % (MOBA_BLOCK // PAGE_SIZE) == 0 and SGU_CHUNK % t_new == 0
    n_s = db * t_new
    assert n_s % SGU_CHUNK == 0 and seq % (8 * MOBA_BLOCK) == 0

    w_in_bf = w_in[0].astype(BF16)
    wo_bf = w_out[0].astype(BF16)
    wmem_bf = w_mem_kv[0].astype(BF16)
    sg_g, sg_b = sgu_ln_g, sgu_ln_b
    moe_w = (w_gate[0].astype(BF16), w_up[0].astype(BF16), w_down[0].astype(BF16),
             ws_gate[0].astype(BF16), ws_up[0].astype(BF16), ws_down[0].astype(BF16))
    wrT = jnp.swapaxes(w_router[0], 0, 1)
    rb_col = router_bias[0][:, None]

    xp = x_prompt[0]
    q, k, v, u, svn, qm = _in_proj(xp, w_in_bf, sg_g, sg_b)
    mkv = _mem_kv(mem_prompt[0], wmem_bf)
    nblk = seq // MOBA_BLOCK
    kmean = _k_block_mean(k)
    k3 = k.astype(BF16).reshape(nblk, MOBA_BLOCK, MOBA_W)
    vT3 = v.astype(BF16).reshape(nblk, MOBA_BLOCK, MOBA_W).transpose(0, 2, 1)
    o_a = _moba_prompt(q.T, k3, vT3, kmean).T
    o_c = _mem_attend_prompt(qm, mkv)
    wm_p = jnp.tril(sgu_w_s[0]).astype(BF16)
    h1 = _mix(xp, o_a, u, svn, o_c, wm_p, _sgu_bias(sgu_b_s[0], SGU_CHUNK), wo_bf, ln1_g, ln1_b)
    y_prompt = _ffn(h1, wrT, rb_col, moe_w, ln2_g, ln2_b)[None]

    n_pg = seq // PAGE_SIZE
    to_pages = lambda z: z.reshape(1, 1, n_pg, PAGE_SIZE, MOBA_HEADS, HEAD_DIM).transpose(0, 1, 2, 4, 3, 5)
    k_prompt, v_prompt = to_pages(k), to_pages(v)
    n_mem = mkv.shape[0]
    mem_k_prompt = mkv[:, :MEM_W].reshape(1, 1, n_mem, MEM_HEADS, HEAD_DIM)
    mem_v_prompt = mkv[:, MEM_W:].reshape(1, 1, n_mem, MEM_HEADS, HEAD_DIM)

    xs = x_sample.reshape(n_s, d_model)
    q, k, v, u, svn, qm = _in_proj(xs, w_in_bf, sg_g, sg_b)
    heads = lambda z: z.reshape(db, t_new, MOBA_HEADS, HEAD_DIM).transpose(0, 2, 1, 3)
    q4, k4, v4 = heads(q), heads(k), heads(v)
    ckT = jnp.swapaxes(cache_k[0], -1, -2)
    cvT = jnp.swapaxes(cache_v[0], -1, -2)
    pt_flat = page_table.reshape(-1)
    scores, kmT = _sample_keys(pt_flat, (q4 * SCALE).astype(BF16), ckT, n_pages)
    o4 = _sample_values(pt_flat, scores, kmT, q4, k4, v4, cvT, n_pages)
    o_a = o4.transpose(0, 2, 1, 3).reshape(n_s, MOBA_W)
    mkT = cache_mem_k[0].transpose(0, 2, 3, 1)
    mvT = cache_mem_v[0].transpose(0, 2, 3, 1)
    o_c = _mem_attend_sample(qm.reshape(db, t_new, MEM_W), mkT, mvT).reshape(n_s, MEM_W)
    wm_s = _sgu_block_diag(sgu_w_s[0], t_new).astype(BF16)
    bz_s = jnp.tile(_sgu_bias(sgu_b_s[0], t_new), (SGU_CHUNK // t_new, 1))
    h1 = _mix(xs, o_a, u, svn, o_c, wm_s, bz_s, wo_bf, ln1_g, ln1_b)
    y_sample = _ffn(h1, wrT, rb_col, moe_w, ln2_g, ln2_b).reshape(db, t_new, d_model)

    k_sample, v_sample = k4[None], v4[None]
    sgu_v_sample = svn.reshape(1, db, t_new, SGU_W)
    return (y_prompt, y_sample, k_prompt, v_prompt, k_sample, v_sample, mem_k_prompt, mem_v_prompt, sgu_v_sample)
```

```python
import functools

import jax
import jax.numpy as jnp
from jax import lax
from jax.experimental import pallas as pl
from jax.experimental.pallas import tpu as pltpu

F32 = jnp.float32
BF16 = jnp.bfloat16

HEAD_DIM = 64
MOBA_HEADS = 8
SGU_GROUPS = 4
MEM_HEADS = 4
MOBA_W = MOBA_HEADS * HEAD_DIM
SGU_W = SGU_GROUPS * HEAD_DIM
MEM_W = MEM_HEADS * HEAD_DIM
MOBA_BLOCK = 256
MOBA_TOPK = 3
Q_BLOCK = 128
SGU_CHUNK = 128
PAGE_SIZE = 128
N_EXPERT_GROUPS = 8
TOPK_GROUPS = 4
TOP_K = 8
ROUTE_SCALE = 2.5
LN_EPS = 1e-5
NEG = -1e30
DEPTH = 1
DEEPNORM_ALPHA = (2 * DEPTH) ** 0.25
SCALE = HEAD_DIM ** -0.5
LOG2E = 1.4426950408889634
ATT_GROUP = 4
SAMPLE_PAGES_PER_STEP = 16
LANES = 128
VMEM_LIMIT = 48 * 1024 * 1024
HIGHEST = lax.Precision.HIGHEST


def _cparams(sem):
    return pltpu.CompilerParams(dimension_semantics=sem, vmem_limit_bytes=VMEM_LIMIT)


def _gelu(x):
    return x * (0.5 * (1.0 + jnp.tanh(0.7978845608028654 * (x + 0.044715 * (x * x * x)))))


def _sigmoid(x):
    return 1.0 / (1.0 + jnp.exp(-x))


def _silu(x):
    return x * _sigmoid(x)


def _layer_norm(x, g, b):
    mu = jnp.mean(x, axis=-1, keepdims=True)
    xc = x - mu
    var = jnp.mean(xc * xc, axis=-1, keepdims=True)
    return xc * lax.rsqrt(var + LN_EPS) * g + b


def _topk_mask(x, k, axis):
    n = x.shape[axis]
    idx = lax.broadcasted_iota(jnp.int32, x.shape, axis).astype(F32)
    sel = jnp.zeros(x.shape, F32)
    for _ in range(k):
        mx = jnp.max(x, axis=axis, keepdims=True)
        first = jnp.min(jnp.where(x == mx, idx, float(n)), axis=axis, keepdims=True)
        hit = idx == first
        sel = jnp.where(hit, 1.0, sel)
        x = jnp.where(hit, -jnp.inf, x)
    return sel


def _dot(a, b):
    return jnp.dot(a, b, preferred_element_type=F32)


def _dot_nt(a, b, precision=None):
    return lax.dot_general(a, b, (((1,), (1,)), ((), ())), precision=precision, preferred_element_type=F32)


def _inproj_body(x_ref, w_ref, g_ref, b_ref, q_ref, k_ref, v_ref, u_ref, sv_ref, qm_ref):
    x = x_ref[...].astype(BF16)
    c0, c1, c2, c3, c4 = MOBA_W, 2 * MOBA_W, 3 * MOBA_W, 3 * MOBA_W + SGU_W, 3 * MOBA_W + 2 * SGU_W
    q_ref[...] = _dot(x, w_ref[:, 0:c0])
    k_ref[...] = _dot(x, w_ref[:, c0:c1])
    v_ref[...] = _dot(x, w_ref[:, c1:c2])
    u_ref[...] = _gelu(_dot(x, w_ref[:, c2:c3]))
    sv_ref[...] = _layer_norm(_gelu(_dot(x, w_ref[:, c3:c4])), g_ref[...], b_ref[...])
    qm_ref[...] = _dot(x, w_ref[:, c4:c4 + MEM_W])


def _in_proj(x, w_in_bf, sgu_g, sgu_b):
    n, d = x.shape
    tm = min(512, n)
    in_w = w_in_bf.shape[1]
    row = lambda w: pl.BlockSpec((tm, w), lambda i: (i, 0))
    full = lambda a: pl.BlockSpec(a.shape, lambda i: (0,) * a.ndim)
    widths = (MOBA_W, MOBA_W, MOBA_W, SGU_W, SGU_W, MEM_W)
    return pl.pallas_call(
        _inproj_body,
        grid=(n // tm,),
        in_specs=[row(d), full(w_in_bf), full(sgu_g), full(sgu_b)],
        out_specs=[row(w) for w in widths],
        out_shape=[jax.ShapeDtypeStruct((n, w), F32) for w in widths],
        compiler_params=_cparams(("parallel",)),
        name="in_proj",
    )(x, w_in_bf, sgu_g, sgu_b)


def _memkv_body(m_ref, w_ref, o_ref):
    o_ref[...] = _dot(m_ref[...].astype(BF16), w_ref[...])


def _mem_kv(mem, w_bf):
    n, _ = mem.shape
    return pl.pallas_call(
        _memkv_body,
        out_shape=jax.ShapeDtypeStruct((n, w_bf.shape[1]), F32),
        compiler_params=pltpu.CompilerParams(vmem_limit_bytes=VMEM_LIMIT),
        name="mem_kv",
    )(mem, w_bf)


def _kmean_body(k_ref, o_ref):
    kb = k_ref[...].reshape(8, MOBA_BLOCK, k_ref.shape[1])
    o_ref[...] = jnp.sum(kb, axis=1) * (1.0 / MOBA_BLOCK)


def _k_block_mean(k):
    s, w = k.shape
    nblk = s // MOBA_BLOCK
    return pl.pallas_call(
        _kmean_body,
        grid=(nblk // 8,),
        in_specs=[pl.BlockSpec((8 * MOBA_BLOCK, w), lambda i: (i, 0))],
        out_specs=pl.BlockSpec((8, w), lambda i: (i, 0)),
        out_shape=jax.ShapeDtypeStruct((nblk, w), F32),
        compiler_params=_cparams(("parallel",)),
        name="k_block_mean",
    )(k)


ATT_PAIR_W = 2 * HEAD_DIM
ATT_ONES = 16
ATT_SLAB = ATT_PAIR_W + ATT_ONES
N_PAIRS = MOBA_HEADS // 2
ATT_VMEM_LIMIT = 58 * 1024 * 1024


def _moba_prompt_body(qT_ref, k_ref, vTg_ref, km_ref, oT_ref, bias_s, dmat_s, qs_s, *, nblk):
    qi = pl.program_id(0)
    own = qi // (MOBA_BLOCK // Q_BLOCK)
    half = qi % (MOBA_BLOCK // Q_BLOCK)
    w2 = 2 * Q_BLOCK
    gk = ATT_GROUP * MOBA_BLOCK

    lane = lax.broadcasted_iota(jnp.int32, (1, w2), 1)
    tl = lane % Q_BLOCK
    sl = lax.broadcasted_iota(jnp.int32, (MOBA_BLOCK, w2), 0)
    n_iota = lax.broadcasted_iota(jnp.int32, (nblk, w2), 0)
    blk_off = (n_iota * MOBA_BLOCK - qi * Q_BLOCK).astype(F32)
    causal = half * Q_BLOCK + tl - sl >= 0
    z = jnp.zeros((HEAD_DIM, Q_BLOCK), F32)

    def pv(j, c, pj):
        res = _dot(vTg_ref[j, c * ATT_SLAB:(c + 1) * ATT_SLAB, :], pj)
        return res[ATT_PAIR_W:ATT_PAIR_W + 1], res[:ATT_PAIR_W]

    state = []
    for c in range(N_PAIRS):
        head = 2 * c + (lane >= Q_BLOCK).astype(jnp.int32)
        slope = jnp.exp2(-(head + 1).astype(F32)) * LOG2E

        @pl.when(qi == 0)
        def _():
            dmat_s[c] = slope * sl.astype(F32)

        qT = qT_ref[c * ATT_PAIR_W:(c + 1) * ATT_PAIR_W, :]
        qbd = jnp.concatenate(
            [jnp.concatenate([qT[:HEAD_DIM], z], axis=1), jnp.concatenate([z, qT[HEAD_DIM:]], axis=1)], axis=0)
        qs_s[c] = (qbd * (SCALE * LOG2E)).astype(BF16)

        gate = jnp.dot(km_ref[:, c * ATT_PAIR_W:(c + 1) * ATT_PAIR_W], qbd, precision=HIGHEST,
                       preferred_element_type=F32)
        gate = jnp.where(n_iota < own, gate, -jnp.inf)
        sel = jnp.logical_and(_topk_mask(gate, MOBA_TOPK, 0) > 0.5, n_iota < own)
        bias_s[c] = jnp.where(sel, slope * blk_off, NEG)

        k_own = k_ref[pl.ds(pl.multiple_of(own * MOBA_BLOCK, MOBA_BLOCK), MOBA_BLOCK),
                      c * ATT_PAIR_W:(c + 1) * ATT_PAIR_W]
        s0 = _dot(k_own, qs_s[c]) + dmat_s[c] - slope * (half * Q_BLOCK).astype(F32)
        s0 = jnp.where(causal, s0, NEG)
        m0 = jnp.max(s0, axis=0, keepdims=True)
        p0 = jnp.exp2(s0 - m0).astype(BF16)
        zero = jnp.zeros_like(p0)
        pj0 = jnp.concatenate([jnp.where(own % ATT_GROUP == g, p0, zero) for g in range(ATT_GROUP)], axis=0)
        l0, acc0 = pv(own // ATT_GROUP, c, pj0)
        state.append((m0, l0, acc0))

    def body(j, carry):
        out = []
        for c in range(N_PAIRS):
            m, l, acc = carry[c]
            parts = []
            m_new = m
            for g in range(ATT_GROUP):
                row0 = pl.multiple_of((j * ATT_GROUP + g) * MOBA_BLOCK, MOBA_BLOCK)
                sg = _dot(k_ref[pl.ds(row0, MOBA_BLOCK), c * ATT_PAIR_W:(c + 1) * ATT_PAIR_W], qs_s[c])
                sg = sg + dmat_s[c] + bias_s[c, pl.ds(j * ATT_GROUP + g, 1), :]
                m_new = jnp.maximum(m_new, jnp.max(sg, axis=0, keepdims=True))
                parts.append(sg)
            a = jnp.exp2(m - m_new)
            res = None
            for g in range(ATT_GROUP):
                pg = jnp.exp2(parts[g] - m_new).astype(BF16)
                vt = vTg_ref[j, c * ATT_SLAB:(c + 1) * ATT_SLAB, g * MOBA_BLOCK:(g + 1) * MOBA_BLOCK]
                res = _dot(vt, pg) if res is None else res + _dot(vt, pg)
            out.append((m_new, a * l + res[ATT_PAIR_W:ATT_PAIR_W + 1], a * acc + res[:ATT_PAIR_W]))
        return tuple(out)

    n_trip = (own + (ATT_GROUP - 1)) // ATT_GROUP
    state = lax.fori_loop(0, n_trip, body, tuple(state))
    for c in range(N_PAIRS):
        _, l, acc = state[c]
        inv = 1.0 / l
        r0 = c * ATT_PAIR_W
        oT_ref[r0:r0 + HEAD_DIM, :] = acc[0:HEAD_DIM, 0:Q_BLOCK] * inv[:, 0:Q_BLOCK]
        oT_ref[r0 + HEAD_DIM:r0 + ATT_PAIR_W, :] = acc[HEAD_DIM:, Q_BLOCK:] * inv[:, Q_BLOCK:]


def _moba_prompt(qT, k_bf, vTg, kmean):
    w, s = qT.shape
    nblk = kmean.shape[0]
    resident = lambda a: pl.BlockSpec(a.shape, lambda i: (0,) * a.ndim, pipeline_mode=pl.Buffered(1))
    return pl.pallas_call(
        functools.partial(_moba_prompt_body, nblk=nblk),
        grid=(s // Q_BLOCK,),
        in_specs=[pl.BlockSpec((w, Q_BLOCK), lambda i: (0, i)), resident(k_bf), resident(vTg), resident(kmean)],
        out_specs=pl.BlockSpec((w, Q_BLOCK), lambda i: (0, i)),
        out_shape=jax.ShapeDtypeStruct((w, s), F32),
        scratch_shapes=[pltpu.VMEM((N_PAIRS, nblk, 2 * Q_BLOCK), F32),
                        pltpu.VMEM((N_PAIRS, MOBA_BLOCK, 2 * Q_BLOCK), F32),
                        pltpu.VMEM((N_PAIRS, ATT_PAIR_W, 2 * Q_BLOCK), BF16)],
        compiler_params=pltpu.CompilerParams(dimension_semantics=("arbitrary",), vmem_limit_bytes=ATT_VMEM_LIMIT),
        name="moba_prompt",
    )(qT, k_bf, vTg, kmean)


def _softmax_rows(logits):
    m = jnp.max(logits, axis=-1, keepdims=True)
    e = jnp.exp(logits - m)
    return e / jnp.sum(e, axis=-1, keepdims=True)


def _mem_prompt_body(qm_ref, mkv_ref, o_ref):
    outs = []
    for h in range(MEM_HEADS):
        lo, hi = h * HEAD_DIM, (h + 1) * HEAD_DIM
        q = (qm_ref[:, lo:hi] * SCALE).astype(BF16)
        mk = mkv_ref[:, lo:hi].astype(BF16)
        mv = mkv_ref[:, MEM_W + lo:MEM_W + hi].astype(BF16)
        p = _softmax_rows(_dot_nt(q, mk))
        outs.append(_dot(p.astype(BF16), mv))
    o_ref[...] = jnp.concatenate(outs, axis=1)


def _mem_attend_prompt(qm, mkv):
    n, w = qm.shape
    tm = min(512, n)
    return pl.pallas_call(
        _mem_prompt_body,
        grid=(n // tm,),
        in_specs=[pl.BlockSpec((tm, w), lambda i: (i, 0)), pl.BlockSpec(mkv.shape, lambda i: (0, 0))],
        out_specs=pl.BlockSpec((tm, w), lambda i: (i, 0)),
        out_shape=jax.ShapeDtypeStruct((n, w), F32),
        compiler_params=_cparams(("parallel",)),
        name="mem_attend_prompt",
    )(qm, mkv)


def _mem_sample_body(qm_ref, mkT_ref, mvT_ref, o_ref, *, sb):
    for i in range(sb):
        outs = []
        for h in range(MEM_HEADS):
            lo, hi = h * HEAD_DIM, (h + 1) * HEAD_DIM
            q = (qm_ref[i, :, lo:hi] * SCALE).astype(BF16)
            p = _softmax_rows(_dot(q, mkT_ref[i, h].astype(BF16)))
            outs.append(_dot_nt(p.astype(BF16), mvT_ref[i, h].astype(BF16)))
        o_ref[i] = jnp.concatenate(outs, axis=1)


def _mem_attend_sample(qm3, mkT, mvT):
    b, t, w = qm3.shape
    sb = min(8, b)
    blk = (sb,) + mkT.shape[1:]
    return pl.pallas_call(
        functools.partial(_mem_sample_body, sb=sb),
        grid=(b // sb,),
        in_specs=[pl.BlockSpec((sb, t, w), lambda i: (i, 0, 0)),
                  pl.BlockSpec(blk, lambda i: (i, 0, 0, 0)), pl.BlockSpec(blk, lambda i: (i, 0, 0, 0))],
        out_specs=pl.BlockSpec((sb, t, w), lambda i: (i, 0, 0)),
        out_shape=jax.ShapeDtypeStruct((b, t, w), F32),
        compiler_params=_cparams(("parallel",)),
        name="mem_attend_sample",
    )(qm3, mkT, mvT)


def _mix_body(x_ref, oa_ref, u_ref, vn_ref, oc_ref, wm_ref, bz_ref, wo_ref, g_ref, b_ref, h_ref, *, tm):
    lane_grp = lax.broadcasted_iota(jnp.int32, (SGU_CHUNK, SGU_W), 1) // HEAD_DIM
    zs = []
    for c in range(tm // SGU_CHUNK):
        vn = vn_ref[c * SGU_CHUNK:(c + 1) * SGU_CHUNK, :].astype(BF16)
        z = bz_ref[...]
        for g in range(SGU_GROUPS):
            z = z + jnp.where(lane_grp == g, _dot(wm_ref[g], vn), 0.0)
        zs.append(z)
    ob = u_ref[...] * jnp.concatenate(zs, axis=0)
    y = _dot(oa_ref[...].astype(BF16), wo_ref[0:MOBA_W, :])
    y = y + _dot(ob.astype(BF16), wo_ref[MOBA_W:MOBA_W + SGU_W, :])
    y = y + _dot(oc_ref[...].astype(BF16), wo_ref[MOBA_W + SGU_W:, :])
    h_ref[...] = _layer_norm(DEEPNORM_ALPHA * x_ref[...] + y, g_ref[...], b_ref[...])


def _mix(x, oa, u, vn, oc, wmat_bf, bz, wo_bf, g, b):
    n, d = x.shape
    tm = min(512, n)
    row = lambda a: pl.BlockSpec((tm, a.shape[1]), lambda i: (i, 0))
    full = lambda a: pl.BlockSpec(a.shape, lambda i: (0,) * a.ndim)
    return pl.pallas_call(
        functools.partial(_mix_body, tm=tm),
        grid=(n // tm,),
        in_specs=[row(x), row(oa), row(u), row(vn), row(oc), full(wmat_bf), full(bz), full(wo_bf), full(g), full(b)],
        out_specs=row(x),
        out_shape=jax.ShapeDtypeStruct((n, d), F32),
        compiler_params=_cparams(("parallel",)),
        name="mix",
    )(x, oa, u, vn, oc, wmat_bf, bz, wo_bf, g, b)


def _rank_rows(x, n_rows, axis):
    idx = lax.broadcasted_iota(jnp.int32, x.shape, axis)
    rank = jnp.zeros(x.shape, F32)
    for j in range(n_rows):
        xj = lax.slice_in_dim(x, j, j + 1, axis=axis)
        tie = jnp.where(idx > j, 1.0, 0.0)
        rank = rank + jnp.where(xj > x, 1.0, jnp.where(xj == x, tie, 0.0))
    return rank


def _router_body(h_ref, wrT_ref, rb_ref, o_ref, *, n_exp):
    tm = h_ref.shape[0]
    gsz = n_exp // N_EXPERT_GROUPS
    scores = _sigmoid(_dot_nt(wrT_ref[...], h_ref[...], precision=HIGHEST))
    biased = scores + rb_ref[...]
    b3 = biased.reshape(N_EXPERT_GROUPS, gsz, tm)
    top2 = jnp.sum(jnp.where(_rank_rows(b3, gsz, 1) < 2.0, b3, 0.0), axis=1, keepdims=True)
    gs = jnp.broadcast_to(top2, b3.shape).reshape(n_exp, tm)
    grp = lax.broadcasted_iota(jnp.int32, (n_exp, tm), 0) // gsz
    grank = jnp.zeros((n_exp, tm), F32)
    for g2 in range(N_EXPERT_GROUPS):
        xg = gs[g2 * gsz:g2 * gsz + 1, :]
        tie = jnp.where(grp > g2, 1.0, 0.0)
        grank = grank + jnp.where(xg > gs, 1.0, jnp.where(xg == gs, tie, 0.0))
    masked = jnp.where(grank < float(TOPK_GROUPS), biased, -jnp.inf)
    w = _topk_mask(masked, TOP_K, 0) * scores
    o_ref[...] = w / jnp.sum(w, axis=0, keepdims=True) * ROUTE_SCALE


def _router(h, wrT, rb_col):
    n, d = h.shape
    n_exp = wrT.shape[0]
    tm = min(512, n)
    return pl.pallas_call(
        functools.partial(_router_body, n_exp=n_exp),
        grid=(n // tm,),
        in_specs=[pl.BlockSpec((tm, d), lambda i: (i, 0)), pl.BlockSpec(wrT.shape, lambda i: (0, 0)),
                  pl.BlockSpec(rb_col.shape, lambda i: (0, 0))],
        out_specs=pl.BlockSpec((n_exp, tm), lambda i: (0, i)),
        out_shape=jax.ShapeDtypeStruct((n_exp, n), F32),
        compiler_params=_cparams(("parallel",)),
        name="router",
    )(h, wrT, rb_col)


def _moe_body(h_ref, gt_ref, wg_ref, wu_ref, wd_ref, sg_ref, su_ref, sd_ref, g_ref, b_ref, o_ref, acc_ref, *, n_exp):
    e = pl.program_id(1)
    hb = h_ref[...].astype(BF16)

    @pl.when(e == 0)
    def _():
        hs = _silu(_dot(hb, sg_ref[...])) * _dot(hb, su_ref[...])
        acc_ref[...] = _dot(hs.astype(BF16), sd_ref[...])

    lane = lax.broadcasted_iota(jnp.int32, gt_ref.shape, 1)
    gcol = jnp.sum(jnp.where(lane == e, gt_ref[...], 0.0), axis=1, keepdims=True)
    hid = _silu(_dot(hb, wg_ref[0])) * _dot(hb, wu_ref[0]) * gcol
    acc_ref[...] += _dot(hid.astype(BF16), wd_ref[0])

    @pl.when(e == n_exp - 1)
    def _():
        o_ref[...] = _layer_norm(DEEPNORM_ALPHA * h_ref[...] + acc_ref[...], g_ref[...], b_ref[...])


def _moe(h, gates, wg, wu, wd, sg, su, sd, g, b):
    n, d = h.shape
    n_exp, _, f = wg.shape
    tm = min(1024, n)
    full = lambda a: pl.BlockSpec(a.shape, lambda i, e: (0,) * a.ndim)
    return pl.pallas_call(
        functools.partial(_moe_body, n_exp=n_exp),
        grid=(n // tm, n_exp),
        in_specs=[pl.BlockSpec((tm, d), lambda i, e: (i, 0)), pl.BlockSpec((tm, n_exp), lambda i, e: (i, 0)),
                  pl.BlockSpec((1, d, f), lambda i, e: (e, 0, 0)), pl.BlockSpec((1, d, f), lambda i, e: (e, 0, 0)),
                  pl.BlockSpec((1, f, d), lambda i, e: (e, 0, 0)),
                  full(sg), full(su), full(sd), full(g), full(b)],
        out_specs=pl.BlockSpec((tm, d), lambda i, e: (i, 0)),
        out_shape=jax.ShapeDtypeStruct((n, d), F32),
        scratch_shapes=[pltpu.VMEM((tm, d), F32)],
        compiler_params=_cparams(("parallel", "arbitrary")),
        name="moe",
    )(h, gates, wg, wu, wd, sg, su, sd, g, b)


def _head_block_diag(q):
    lane_head = lax.broadcasted_iota(jnp.int32, q.shape, 1) // HEAD_DIM
    return jnp.concatenate([jnp.where(lane_head == h, q, 0.0) for h in range(MOBA_HEADS)], axis=0)


def _sample_keys_body(pt_ref, q_ref, *refs, ppstep):
    pages, (sc_ref, ks_ref) = refs[:ppstep], refs[ppstep:]
    qbd = (_head_block_diag(q_ref[0]) * SCALE).astype(BF16)
    ones = jnp.ones((8, PAGE_SIZE), BF16)
    for i in range(ppstep):
        kt = pages[i][0].reshape(MOBA_W, PAGE_SIZE)
        hi = kt.astype(BF16)
        lo = (kt - hi.astype(F32)).astype(BF16)
        sc_ref[0, :, i * PAGE_SIZE:(i + 1) * PAGE_SIZE] = _dot(qbd, hi)
        ks_ref[0, i:i + 1, :] = (_dot_nt(ones, hi) + _dot_nt(ones, lo))[0:1]


def _page_spec(n_pages, ppstep, i):
    return pl.BlockSpec((1, MOBA_HEADS, HEAD_DIM, PAGE_SIZE),
                        lambda b, s, pt: (pt[b * n_pages + s * ppstep + i], 0, 0, 0))


def _sample_keys(pt_flat, q3, ckT, n_pages):
    b, t, w = q3.shape
    ppstep = min(SAMPLE_PAGES_PER_STEP, n_pages)
    ht = MOBA_HEADS * t
    grid_spec = pltpu.PrefetchScalarGridSpec(
        num_scalar_prefetch=1,
        grid=(b, n_pages // ppstep),
        in_specs=[pl.BlockSpec((1, t, w), lambda b_, s, pt: (b_, 0, 0))]
        + [_page_spec(n_pages, ppstep, i) for i in range(ppstep)],
        out_specs=[pl.BlockSpec((1, ht, ppstep * PAGE_SIZE), lambda b_, s, pt: (b_, 0, s)),
                   pl.BlockSpec((1, ppstep, w), lambda b_, s, pt: (b_, s, 0))],
    )
    return pl.pallas_call(
        functools.partial(_sample_keys_body, ppstep=ppstep),
        grid_spec=grid_spec,
        out_shape=[jax.ShapeDtypeStruct((b, ht, n_pages * PAGE_SIZE), F32),
                   jax.ShapeDtypeStruct((b, n_pages, w), F32)],
        compiler_params=_cparams(("arbitrary", "arbitrary")),
        name="moba_sample_keys",
    )(pt_flat, q3, *([ckT] * ppstep))


def _sample_values_body(pt_ref, sc_ref, ks_ref, q_ref, kn_ref, vn_ref, *refs, ppstep, n_pages, t_new):
    pages, o_ref, (lg_s, p_s, l_s, acc_s) = refs[:ppstep], refs[ppstep], refs[ppstep + 1:]
    s = pl.program_id(1)
    ppb = MOBA_BLOCK // PAGE_SIZE
    n_fp = n_pages // ppb
    n_sel = min(MOBA_TOPK, n_fp)
    past = n_pages * PAGE_SIZE
    ht = MOBA_HEADS * t_new
    bps = ppstep // ppb

    @pl.when(s == 0)
    def _():
        qbd = _head_block_diag(q_ref[0])
        r = lax.broadcasted_iota(jnp.int32, (LANES, n_pages), 0)
        c = lax.broadcasted_iota(jnp.int32, (LANES, n_pages), 1)
        pair = jnp.where(c // ppb == r, 1.0, 0.0)
        ksb = jnp.dot(pair, ks_ref[0], precision=HIGHEST, preferred_element_type=F32)
        gate = _dot_nt(qbd, ksb, precision=HIGHEST) * (1.0 / MOBA_BLOCK)
        blane = lax.broadcasted_iota(jnp.int32, (ht, LANES), 1)
        sel = _topk_mask(jnp.where(blane < n_fp, gate, -jnp.inf), n_sel, 1)

        row = lax.broadcasted_iota(jnp.int32, (ht, 1), 0)
        slope = jnp.exp2(-((row // t_new) + 1).astype(F32))
        t_col = row % t_new
        qs = (qbd * SCALE).astype(BF16)
        pad = jnp.zeros((LANES - t_new, MOBA_W), F32)
        knew = jnp.concatenate([kn_ref[0], pad], axis=0).astype(BF16)
        vnew = jnp.concatenate([vn_ref[0], pad], axis=0).astype(BF16)
        l_own = _dot_nt(qs, knew) - slope * (t_col - blane).astype(F32)
        l_own = jnp.where(blane <= t_col, l_own, NEG)
        m = jnp.max(l_own, axis=1, keepdims=True)
        pos = lax.broadcasted_iota(jnp.int32, (ht, MOBA_BLOCK), 1)
        for n in range(n_fp):
            cols = slice(n * MOBA_BLOCK, (n + 1) * MOBA_BLOCK)
            dist = (past - n * MOBA_BLOCK) + t_col - pos
            ln = sc_ref[0, :, cols] - slope * dist.astype(F32)
            ln = jnp.where(sel[:, n:n + 1] > 0.5, ln, NEG)
            m = jnp.maximum(m, jnp.max(ln, axis=1, keepdims=True))
            lg_s[:, cols] = ln
        p_own = jnp.exp(l_own - m)
        lsum = jnp.sum(p_own, axis=1, keepdims=True)
        for n in range(n_fp):
            pn = jnp.exp(lg_s[:, n * MOBA_BLOCK:(n + 1) * MOBA_BLOCK] - m)
            lsum = lsum + jnp.sum(pn, axis=1, keepdims=True)
            p_s[n // bps, :, (n % bps) * MOBA_BLOCK:(n % bps + 1) * MOBA_BLOCK] = pn.astype(BF16)
        l_s[...] = jnp.broadcast_to(lsum, (ht, LANES))
        acc_s[...] = _dot(p_own.astype(BF16), vnew)

    vt = jnp.concatenate([pages[i][0].reshape(MOBA_W, PAGE_SIZE).astype(BF16) for i in range(ppstep)], axis=1)
    acc_s[...] += _dot_nt(p_s[s], vt)

    @pl.when(s == pl.num_programs(1) - 1)
    def _():
        o_all = acc_s[...] / l_s[:, 0:1]
        lane_head = lax.broadcasted_iota(jnp.int32, (t_new, MOBA_W), 1) // HEAD_DIM
        out = jnp.zeros((t_new, MOBA_W), F32)
        for h in range(MOBA_HEADS):
            out = out + jnp.where(lane_head == h, o_all[h * t_new:(h + 1) * t_new], 0.0)
        o_ref[0] = out


def _sample_values(pt_flat, scores, ksum, q3, kn3, vn3, cvT, n_pages):
    b, t, w = q3.shape
    ppstep = min(SAMPLE_PAGES_PER_STEP, n_pages)
    ht = MOBA_HEADS * t
    n_steps = n_pages // ppstep
    small = lambda a: pl.BlockSpec((1,) + a.shape[1:], lambda b_, s, pt: (b_,) + (0,) * (a.ndim - 1))
    grid_spec = pltpu.PrefetchScalarGridSpec(
        num_scalar_prefetch=1,
        grid=(b, n_steps),
        in_specs=[small(scores), small(ksum), small(q3), small(kn3), small(vn3)]
        + [_page_spec(n_pages, ppstep, i) for i in range(ppstep)],
        out_specs=pl.BlockSpec((1, t, w), lambda b_, s, pt: (b_, 0, 0)),
        scratch_shapes=[pltpu.VMEM((ht, n_pages * PAGE_SIZE), F32),
                        pltpu.VMEM((n_steps, ht, ppstep * PAGE_SIZE), BF16),
                        pltpu.VMEM((ht, LANES), F32),
                        pltpu.VMEM((ht, w), F32)],
    )
    return pl.pallas_call(
        functools.partial(_sample_values_body, ppstep=ppstep, n_pages=n_pages, t_new=t),
        grid_spec=grid_spec,
        out_shape=jax.ShapeDtypeStruct((b, t, w), F32),
        compiler_params=_cparams(("arbitrary", "arbitrary")),
        name="moba_sample_values",
    )(pt_flat, scores, ksum, q3, kn3, vn3, *([cvT] * ppstep))


def _sgu_block_diag(w_s, t_new):
    g = w_s.shape[0]
    reps = SGU_CHUNK // t_new
    small = jnp.tril(w_s)[:, :t_new, :t_new]
    eye = jnp.eye(reps, dtype=w_s.dtype)
    return jnp.einsum("rq,gts->grtqs", eye, small).reshape(g, SGU_CHUNK, SGU_CHUNK)


def _sgu_bias(b_s, rows):
    return jnp.repeat(b_s[:, :rows].T, HEAD_DIM, axis=1)


def _ffn(h1, wrT, rb_col, moe_w, g, b):
    gates = _router(h1, wrT, rb_col).T
    return _moe(h1, gates, *moe_w, g, b)


def kernel(x_prompt, x_sample, mem_prompt, cache_k, cache_v, page_table, cache_mem_k, cache_mem_v, w_in, w_out, w_mem_kv, sgu_ln_g, sgu_ln_b, sgu_w_s, sgu_b_s, ln1_g, ln1_b, w_router, router_bias, w_gate, w_up, w_down, ws_gate, ws_up, ws_down, ln2_g, ln2_b):
    assert w_in.shape[0] == DEPTH and x_prompt.shape[0] == 1
    _, seq, d_model = x_prompt.shape
    db, t_new, _ = x_sample.shape
    n_pages = page_table.shape[1]
    assert n_pages % (MOBA_BLOCK // PAGE_SIZE) == 0 and SGU_CHUNK % t_new == 0
    n_s = db * t_new
    assert n_s % SGU_CHUNK == 0 and seq % (8 * MOBA_BLOCK) == 0 and seq % (ATT_GROUP * MOBA_BLOCK) == 0
    assert n_pages % min(SAMPLE_PAGES_PER_STEP, n_pages) == 0

    w_in_bf = w_in[0].astype(BF16)
    wo_bf = w_out[0].astype(BF16)
    wmem_bf = w_mem_kv[0].astype(BF16)
    sg_g, sg_b = sgu_ln_g, sgu_ln_b
    moe_w = (w_gate[0].astype(BF16), w_up[0].astype(BF16), w_down[0].astype(BF16),
             ws_gate[0].astype(BF16), ws_up[0].astype(BF16), ws_down[0].astype(BF16))
    wrT = jnp.swapaxes(w_router[0], 0, 1)
    rb_col = router_bias[0][:, None]

    xp = x_prompt[0]
    q, k, v, u, svn, qm = _in_proj(xp, w_in_bf, sg_g, sg_b)
    mkv = _mem_kv(mem_prompt[0], wmem_bf)
    nblk = seq // MOBA_BLOCK
    kmean = _k_block_mean(k)
    gk = ATT_GROUP * MOBA_BLOCK
    ngrp = seq // gk
    vT = v.astype(BF16).reshape(ngrp, gk, N_PAIRS, ATT_PAIR_W).transpose(0, 2, 3, 1)
    vTg = jnp.concatenate([vT, jnp.ones((ngrp, N_PAIRS, ATT_ONES, gk), BF16)], axis=2)
    vTg = vTg.reshape(ngrp, N_PAIRS * ATT_SLAB, gk)
    o_a = _moba_prompt(q.T, k.astype(BF16), vTg, kmean).T
    o_c = _mem_attend_prompt(qm, mkv)
    wm_p = jnp.tril(sgu_w_s[0]).astype(BF16)
    h1 = _mix(xp, o_a, u, svn, o_c, wm_p, _sgu_bias(sgu_b_s[0], SGU_CHUNK), wo_bf, ln1_g, ln1_b)
    y_prompt = _ffn(h1, wrT, rb_col, moe_w, ln2_g, ln2_b)[None]

    n_pg = seq // PAGE_SIZE
    to_pages = lambda z: z.reshape(1, 1, n_pg, PAGE_SIZE, MOBA_HEADS, HEAD_DIM).transpose(0, 1, 2, 4, 3, 5)
    k_prompt, v_prompt = to_pages(k), to_pages(v)
    n_mem = mkv.shape[0]
    mem_k_prompt = mkv[:, :MEM_W].reshape(1, 1, n_mem, MEM_HEADS, HEAD_DIM)
    mem_v_prompt = mkv[:, MEM_W:].reshape(1, 1, n_mem, MEM_HEADS, HEAD_DIM)

    xs = x_sample.reshape(n_s, d_model)
    q, k, v, u, svn, qm = _in_proj(xs, w_in_bf, sg_g, sg_b)
    heads = lambda z: z.reshape(db, t_new, MOBA_HEADS, HEAD_DIM).transpose(0, 2, 1, 3)
    k4, v4 = heads(k), heads(v)
    ckT = jnp.swapaxes(cache_k[0], -1, -2)
    cvT = jnp.swapaxes(cache_v[0], -1, -2)
    pt_flat = page_table.reshape(-1)
    three = lambda z: z.reshape(db, t_new, MOBA_W)
    scores, ksum = _sample_keys(pt_flat, three(q), ckT, n_pages)
    o_a = _sample_values(pt_flat, scores, ksum, three(q), three(k), three(v), cvT, n_pages).reshape(n_s, MOBA_W)
    mkT = cache_mem_k[0].transpose(0, 2, 3, 1)
    mvT = cache_mem_v[0].transpose(0, 2, 3, 1)
    o_c = _mem_attend_sample(qm.reshape(db, t_new, MEM_W), mkT, mvT).reshape(n_s, MEM_W)
    wm_s = _sgu_block_diag(sgu_w_s[0], t_new).astype(BF16)
    bz_s = jnp.tile(_sgu_bias(sgu_b_s[0], t_new), (SGU_CHUNK // t_new, 1))
    h1 = _mix(xs, o_a, u, svn, o_c, wm_s, bz_s, wo_bf, ln1_g, ln1_b)
    y_sample = _ffn(h1, wrT, rb_col, moe_w, ln2_g, ln2_b).reshape(db, t_new, d_model)

    k_sample, v_sample = k4[None], v4[None]
    sgu_v_sample = svn.reshape(1, db, t_new, SGU_W)
    return (y_prompt, y_sample, k_prompt, v_prompt, k_sample, v_sample, mem_k_prompt, mem_v_prompt, sgu_v_sample)
```

```python
import functools

import jax
import jax.numpy as jnp
from jax import lax
from jax.experimental import pallas as pl
from jax.experimental.pallas import tpu as pltpu

F32 = jnp.float32
BF16 = jnp.bfloat16

HEAD_DIM = 64
MOBA_HEADS = 8
SGU_GROUPS = 4
MEM_HEADS = 4
MOBA_W = MOBA_HEADS * HEAD_DIM
SGU_W = SGU_GROUPS * HEAD_DIM
MEM_W = MEM_HEADS * HEAD_DIM
MOBA_BLOCK = 256
MOBA_TOPK = 3
Q_BLOCK = 128
SGU_CHUNK = 128
PAGE_SIZE = 128
N_EXPERT_GROUPS = 8
TOPK_GROUPS = 4
TOP_K = 8
ROUTE_SCALE = 2.5
LN_EPS = 1e-5
NEG = -1e30
DEPTH = 1
DEEPNORM_ALPHA = (2 * DEPTH) ** 0.25
SCALE = HEAD_DIM ** -0.5
LOG2E = 1.4426950408889634
ATT_GROUP = 8
SAMPLE_PAGES_PER_STEP = 32
LANES = 128
VMEM_LIMIT = 48 * 1024 * 1024
HIGHEST = lax.Precision.HIGHEST


def _cparams(sem):
    return pltpu.CompilerParams(dimension_semantics=sem, vmem_limit_bytes=VMEM_LIMIT)


def _gelu(x):
    return x * (0.5 * (1.0 + jnp.tanh(0.7978845608028654 * (x + 0.044715 * (x * x * x)))))


def _sigmoid(x):
    return 1.0 / (1.0 + jnp.exp(-x))


def _silu(x):
    return x * _sigmoid(x)


def _layer_norm(x, g, b):
    mu = jnp.mean(x, axis=-1, keepdims=True)
    xc = x - mu
    var = jnp.mean(xc * xc, axis=-1, keepdims=True)
    return xc * lax.rsqrt(var + LN_EPS) * g + b


def _topk_mask(x, k, axis):
    n = x.shape[axis]
    idx = lax.broadcasted_iota(jnp.int32, x.shape, axis).astype(F32)
    sel = jnp.zeros(x.shape, F32)
    for _ in range(k):
        mx = jnp.max(x, axis=axis, keepdims=True)
        first = jnp.min(jnp.where(x == mx, idx, float(n)), axis=axis, keepdims=True)
        hit = idx == first
        sel = jnp.where(hit, 1.0, sel)
        x = jnp.where(hit, -jnp.inf, x)
    return sel


def _dot(a, b):
    return jnp.dot(a, b, preferred_element_type=F32)


def _dot_nt(a, b, precision=None):
    return lax.dot_general(a, b, (((1,), (1,)), ((), ())), precision=precision, preferred_element_type=F32)


def _inproj_body(x_ref, w_ref, g_ref, b_ref, q_ref, k_ref, v_ref, u_ref, sv_ref, qm_ref):
    x = x_ref[...].astype(BF16)
    c0, c1, c2, c3, c4 = MOBA_W, 2 * MOBA_W, 3 * MOBA_W, 3 * MOBA_W + SGU_W, 3 * MOBA_W + 2 * SGU_W
    q_ref[...] = _dot(x, w_ref[:, 0:c0])
    k_ref[...] = _dot(x, w_ref[:, c0:c1])
    v_ref[...] = _dot(x, w_ref[:, c1:c2])
    u_ref[...] = _gelu(_dot(x, w_ref[:, c2:c3]))
    sv_ref[...] = _layer_norm(_gelu(_dot(x, w_ref[:, c3:c4])), g_ref[...], b_ref[...])
    qm_ref[...] = _dot(x, w_ref[:, c4:c4 + MEM_W])


def _in_proj(x, w_in_bf, sgu_g, sgu_b):
    n, d = x.shape
    tm = min(512, n)
    in_w = w_in_bf.shape[1]
    row = lambda w: pl.BlockSpec((tm, w), lambda i: (i, 0))
    full = lambda a: pl.BlockSpec(a.shape, lambda i: (0,) * a.ndim)
    widths = (MOBA_W, MOBA_W, MOBA_W, SGU_W, SGU_W, MEM_W)
    return pl.pallas_call(
        _inproj_body,
        grid=(n // tm,),
        in_specs=[row(d), full(w_in_bf), full(sgu_g), full(sgu_b)],
        out_specs=[row(w) for w in widths],
        out_shape=[jax.ShapeDtypeStruct((n, w), F32) for w in widths],
        compiler_params=_cparams(("parallel",)),
        name="in_proj",
    )(x, w_in_bf, sgu_g, sgu_b)


def _memkv_body(m_ref, w_ref, o_ref):
    o_ref[...] = _dot(m_ref[...].astype(BF16), w_ref[...])


def _mem_kv(mem, w_bf):
    n, _ = mem.shape
    return pl.pallas_call(
        _memkv_body,
        out_shape=jax.ShapeDtypeStruct((n, w_bf.shape[1]), F32),
        compiler_params=pltpu.CompilerParams(vmem_limit_bytes=VMEM_LIMIT),
        name="mem_kv",
    )(mem, w_bf)


def _kmean_body(k_ref, o_ref):
    kb = k_ref[...].reshape(8, MOBA_BLOCK, k_ref.shape[1])
    o_ref[...] = jnp.sum(kb, axis=1) * (1.0 / MOBA_BLOCK)


def _k_block_mean(k):
    s, w = k.shape
    nblk = s // MOBA_BLOCK
    return pl.pallas_call(
        _kmean_body,
        grid=(nblk // 8,),
        in_specs=[pl.BlockSpec((8 * MOBA_BLOCK, w), lambda i: (i, 0))],
        out_specs=pl.BlockSpec((8, w), lambda i: (i, 0)),
        out_shape=jax.ShapeDtypeStruct((nblk, w), F32),
        compiler_params=_cparams(("parallel",)),
        name="k_block_mean",
    )(k)


ATT_PAIR_W = 2 * HEAD_DIM
ATT_ONES = 16
ATT_SLAB = ATT_PAIR_W + ATT_ONES
N_PAIRS = MOBA_HEADS // 2
ATT_VMEM_LIMIT = 58 * 1024 * 1024


def _moba_prompt_body(qT_ref, k_ref, vTg_ref, km_ref, oT_ref, bias_s, dmat_s, qs_s, *, nblk):
    qi = pl.program_id(0)
    own = qi // (MOBA_BLOCK // Q_BLOCK)
    half = qi % (MOBA_BLOCK // Q_BLOCK)
    w2 = 2 * Q_BLOCK
    gk = ATT_GROUP * MOBA_BLOCK

    lane = lax.broadcasted_iota(jnp.int32, (1, w2), 1)
    tl = lane % Q_BLOCK
    sl = lax.broadcasted_iota(jnp.int32, (MOBA_BLOCK, w2), 0)
    n_iota = lax.broadcasted_iota(jnp.int32, (nblk, w2), 0)
    blk_off = (n_iota * MOBA_BLOCK - qi * Q_BLOCK).astype(F32)
    causal = half * Q_BLOCK + tl - sl >= 0
    z = jnp.zeros((HEAD_DIM, Q_BLOCK), F32)

    def pv(j, c, pj):
        res = _dot(vTg_ref[j, c * ATT_SLAB:(c + 1) * ATT_SLAB, :], pj)
        return res[ATT_PAIR_W:ATT_PAIR_W + 1], res[:ATT_PAIR_W]

    state = []
    for c in range(N_PAIRS):
        head = 2 * c + (lane >= Q_BLOCK).astype(jnp.int32)
        slope = jnp.exp2(-(head + 1).astype(F32)) * LOG2E

        @pl.when(qi == 0)
        def _():
            dmat_s[c] = slope * sl.astype(F32)

        qT = qT_ref[c * ATT_PAIR_W:(c + 1) * ATT_PAIR_W, :]
        qbd = jnp.concatenate(
            [jnp.concatenate([qT[:HEAD_DIM], z], axis=1), jnp.concatenate([z, qT[HEAD_DIM:]], axis=1)], axis=0)
        qs_s[c] = (qbd * (SCALE * LOG2E)).astype(BF16)

        gate = jnp.dot(km_ref[:, c * ATT_PAIR_W:(c + 1) * ATT_PAIR_W], qbd, precision=HIGHEST,
                       preferred_element_type=F32)
        gate = jnp.where(n_iota < own, gate, -jnp.inf)
        sel = jnp.logical_and(_topk_mask(gate, MOBA_TOPK, 0) > 0.5, n_iota < own)
        bias_s[c] = jnp.where(sel, slope * blk_off, NEG)

        k_own = k_ref[pl.ds(pl.multiple_of(own * MOBA_BLOCK, MOBA_BLOCK), MOBA_BLOCK),
                      c * ATT_PAIR_W:(c + 1) * ATT_PAIR_W]
        s0 = _dot(k_own, qs_s[c]) + dmat_s[c] - slope * (half * Q_BLOCK).astype(F32)
        s0 = jnp.where(causal, s0, NEG)
        m0 = jnp.max(s0, axis=0, keepdims=True)
        p0 = jnp.exp2(s0 - m0).astype(BF16)
        zero = jnp.zeros_like(p0)
        pj0 = jnp.concatenate([jnp.where(own % ATT_GROUP == g, p0, zero) for g in range(ATT_GROUP)], axis=0)
        l0, acc0 = pv(own // ATT_GROUP, c, pj0)
        state.append((m0, l0, acc0))

    def body(j, carry):
        out = []
        for c in range(N_PAIRS):
            m, l, acc = carry[c]
            parts = []
            m_new = m
            for g in range(ATT_GROUP):
                row0 = pl.multiple_of((j * ATT_GROUP + g) * MOBA_BLOCK, MOBA_BLOCK)
                sg = _dot(k_ref[pl.ds(row0, MOBA_BLOCK), c * ATT_PAIR_W:(c + 1) * ATT_PAIR_W], qs_s[c])
                sg = sg + dmat_s[c] + bias_s[c, pl.ds(j * ATT_GROUP + g, 1), :]
                m_new = jnp.maximum(m_new, jnp.max(sg, axis=0, keepdims=True))
                parts.append(sg)
            a = jnp.exp2(m - m_new)
            res = None
            for g in range(ATT_GROUP):
                pg = jnp.exp2(parts[g] - m_new).astype(BF16)
                vt = vTg_ref[j, c * ATT_SLAB:(c + 1) * ATT_SLAB, g * MOBA_BLOCK:(g + 1) * MOBA_BLOCK]
                res = _dot(vt, pg) if res is None else res + _dot(vt, pg)
            out.append((m_new, a * l + res[ATT_PAIR_W:ATT_PAIR_W + 1], a * acc + res[:ATT_PAIR_W]))
        return tuple(out)

    n_trip = (own + (ATT_GROUP - 1)) // ATT_GROUP
    state = lax.fori_loop(0, n_trip, body, tuple(state))
    for c in range(N_PAIRS):
        _, l, acc = state[c]
        inv = 1.0 / l
        r0 = c * ATT_PAIR_W
        oT_ref[r0:r0 + HEAD_DIM, :] = acc[0:HEAD_DIM, 0:Q_BLOCK] * inv[:, 0:Q_BLOCK]
        oT_ref[r0 + HEAD_DIM:r0 + ATT_PAIR_W, :] = acc[HEAD_DIM:, Q_BLOCK:] * inv[:, Q_BLOCK:]


def _moba_prompt(qT, k_bf, vTg, kmean):
    w, s = qT.shape
    nblk = kmean.shape[0]
    resident = lambda a: pl.BlockSpec(a.shape, lambda i: (0,) * a.ndim, pipeline_mode=pl.Buffered(1))
    return pl.pallas_call(
        functools.partial(_moba_prompt_body, nblk=nblk),
        grid=(s // Q_BLOCK,),
        in_specs=[pl.BlockSpec((w, Q_BLOCK), lambda i: (0, i)), resident(k_bf), resident(vTg), resident(kmean)],
        out_specs=pl.BlockSpec((w, Q_BLOCK), lambda i: (0, i)),
        out_shape=jax.ShapeDtypeStruct((w, s), F32),
        scratch_shapes=[pltpu.VMEM((N_PAIRS, nblk, 2 * Q_BLOCK), F32),
                        pltpu.VMEM((N_PAIRS, MOBA_BLOCK, 2 * Q_BLOCK), F32),
                        pltpu.VMEM((N_PAIRS, ATT_PAIR_W, 2 * Q_BLOCK), BF16)],
        compiler_params=pltpu.CompilerParams(dimension_semantics=("arbitrary",), vmem_limit_bytes=ATT_VMEM_LIMIT),
        name="moba_prompt",
    )(qT, k_bf, vTg, kmean)


def _softmax_rows(logits):
    m = jnp.max(logits, axis=-1, keepdims=True)
    e = jnp.exp(logits - m)
    return e / jnp.sum(e, axis=-1, keepdims=True)


def _mem_prompt_body(qm_ref, mkv_ref, o_ref):
    outs = []
    for h in range(MEM_HEADS):
        lo, hi = h * HEAD_DIM, (h + 1) * HEAD_DIM
        q = (qm_ref[:, lo:hi] * SCALE).astype(BF16)
        mk = mkv_ref[:, lo:hi].astype(BF16)
        mv = mkv_ref[:, MEM_W + lo:MEM_W + hi].astype(BF16)
        p = _softmax_rows(_dot_nt(q, mk))
        outs.append(_dot(p.astype(BF16), mv))
    o_ref[...] = jnp.concatenate(outs, axis=1)


def _mem_attend_prompt(qm, mkv):
    n, w = qm.shape
    tm = min(512, n)
    return pl.pallas_call(
        _mem_prompt_body,
        grid=(n // tm,),
        in_specs=[pl.BlockSpec((tm, w), lambda i: (i, 0)), pl.BlockSpec(mkv.shape, lambda i: (0, 0))],
        out_specs=pl.BlockSpec((tm, w), lambda i: (i, 0)),
        out_shape=jax.ShapeDtypeStruct((n, w), F32),
        compiler_params=_cparams(("parallel",)),
        name="mem_attend_prompt",
    )(qm, mkv)


def _mem_sample_body(qm_ref, mkT_ref, mvT_ref, o_ref, *, sb):
    for i in range(sb):
        outs = []
        for h in range(MEM_HEADS):
            lo, hi = h * HEAD_DIM, (h + 1) * HEAD_DIM
            q = (qm_ref[i, :, lo:hi] * SCALE).astype(BF16)
            p = _softmax_rows(_dot(q, mkT_ref[i, h].astype(BF16)))
            outs.append(_dot_nt(p.astype(BF16), mvT_ref[i, h].astype(BF16)))
        o_ref[i] = jnp.concatenate(outs, axis=1)


def _mem_attend_sample(qm3, mkT, mvT):
    b, t, w = qm3.shape
    sb = min(8, b)
    blk = (sb,) + mkT.shape[1:]
    return pl.pallas_call(
        functools.partial(_mem_sample_body, sb=sb),
        grid=(b // sb,),
        in_specs=[pl.BlockSpec((sb, t, w), lambda i: (i, 0, 0)),
                  pl.BlockSpec(blk, lambda i: (i, 0, 0, 0)), pl.BlockSpec(blk, lambda i: (i, 0, 0, 0))],
        out_specs=pl.BlockSpec((sb, t, w), lambda i: (i, 0, 0)),
        out_shape=jax.ShapeDtypeStruct((b, t, w), F32),
        compiler_params=_cparams(("parallel",)),
        name="mem_attend_sample",
    )(qm3, mkT, mvT)


def _mix_body(x_ref, oa_ref, u_ref, vn_ref, oc_ref, wm_ref, bz_ref, wo_ref, g_ref, b_ref, h_ref, *, tm):
    lane_grp = lax.broadcasted_iota(jnp.int32, (SGU_CHUNK, SGU_W), 1) // HEAD_DIM
    zs = []
    for c in range(tm // SGU_CHUNK):
        vn = vn_ref[c * SGU_CHUNK:(c + 1) * SGU_CHUNK, :].astype(BF16)
        z = bz_ref[...]
        for g in range(SGU_GROUPS):
            z = z + jnp.where(lane_grp == g, _dot(wm_ref[g], vn), 0.0)
        zs.append(z)
    ob = u_ref[...] * jnp.concatenate(zs, axis=0)
    y = _dot(oa_ref[...].astype(BF16), wo_ref[0:MOBA_W, :])
    y = y + _dot(ob.astype(BF16), wo_ref[MOBA_W:MOBA_W + SGU_W, :])
    y = y + _dot(oc_ref[...].astype(BF16), wo_ref[MOBA_W + SGU_W:, :])
    h_ref[...] = _layer_norm(DEEPNORM_ALPHA * x_ref[...] + y, g_ref[...], b_ref[...])


def _mix(x, oa, u, vn, oc, wmat_bf, bz, wo_bf, g, b):
    n, d = x.shape
    tm = min(512, n)
    row = lambda a: pl.BlockSpec((tm, a.shape[1]), lambda i: (i, 0))
    full = lambda a: pl.BlockSpec(a.shape, lambda i: (0,) * a.ndim)
    return pl.pallas_call(
        functools.partial(_mix_body, tm=tm),
        grid=(n // tm,),
        in_specs=[row(x), row(oa), row(u), row(vn), row(oc), full(wmat_bf), full(bz), full(wo_bf), full(g), full(b)],
        out_specs=row(x),
        out_shape=jax.ShapeDtypeStruct((n, d), F32),
        compiler_params=_cparams(("parallel",)),
        name="mix",
    )(x, oa, u, vn, oc, wmat_bf, bz, wo_bf, g, b)


def _rank_rows(x, n_rows, axis):
    idx = lax.broadcasted_iota(jnp.int32, x.shape, axis)
    rank = jnp.zeros(x.shape, F32)
    for j in range(n_rows):
        xj = lax.slice_in_dim(x, j, j + 1, axis=axis)
        tie = jnp.where(idx > j, 1.0, 0.0)
        rank = rank + jnp.where(xj > x, 1.0, jnp.where(xj == x, tie, 0.0))
    return rank


def _router_body(h_ref, wrT_ref, rb_ref, o_ref, *, n_exp):
    tm = h_ref.shape[0]
    gsz = n_exp // N_EXPERT_GROUPS
    scores = _sigmoid(_dot_nt(wrT_ref[...], h_ref[...], precision=HIGHEST))
    biased = scores + rb_ref[...]
    b3 = biased.reshape(N_EXPERT_GROUPS, gsz, tm)
    top2 = jnp.sum(jnp.where(_rank_rows(b3, gsz, 1) < 2.0, b3, 0.0), axis=1, keepdims=True)
    gs = jnp.broadcast_to(top2, b3.shape).reshape(n_exp, tm)
    grp = lax.broadcasted_iota(jnp.int32, (n_exp, tm), 0) // gsz
    grank = jnp.zeros((n_exp, tm), F32)
    for g2 in range(N_EXPERT_GROUPS):
        xg = gs[g2 * gsz:g2 * gsz + 1, :]
        tie = jnp.where(grp > g2, 1.0, 0.0)
        grank = grank + jnp.where(xg > gs, 1.0, jnp.where(xg == gs, tie, 0.0))
    masked = jnp.where(grank < float(TOPK_GROUPS), biased, -jnp.inf)
    w = _topk_mask(masked, TOP_K, 0) * scores
    o_ref[...] = w / jnp.sum(w, axis=0, keepdims=True) * ROUTE_SCALE


def _router(h, wrT, rb_col):
    n, d = h.shape
    n_exp = wrT.shape[0]
    tm = min(512, n)
    return pl.pallas_call(
        functools.partial(_router_body, n_exp=n_exp),
        grid=(n // tm,),
        in_specs=[pl.BlockSpec((tm, d), lambda i: (i, 0)), pl.BlockSpec(wrT.shape, lambda i: (0, 0)),
                  pl.BlockSpec(rb_col.shape, lambda i: (0, 0))],
        out_specs=pl.BlockSpec((n_exp, tm), lambda i: (0, i)),
        out_shape=jax.ShapeDtypeStruct((n_exp, n), F32),
        compiler_params=_cparams(("parallel",)),
        name="router",
    )(h, wrT, rb_col)


def _moe_body(h_ref, gt_ref, wg_ref, wu_ref, wd_ref, sg_ref, su_ref, sd_ref, g_ref, b_ref, o_ref, acc_ref, *, n_exp):
    e = pl.program_id(1)
    hb = h_ref[...].astype(BF16)

    @pl.when(e == 0)
    def _():
        hs = _silu(_dot(hb, sg_ref[...])) * _dot(hb, su_ref[...])
        acc_ref[...] = _dot(hs.astype(BF16), sd_ref[...])

    lane = lax.broadcasted_iota(jnp.int32, gt_ref.shape, 1)
    gcol = jnp.sum(jnp.where(lane == e, gt_ref[...], 0.0), axis=1, keepdims=True)
    hid = _silu(_dot(hb, wg_ref[0].astype(BF16))) * _dot(hb, wu_ref[0].astype(BF16)) * gcol
    acc_ref[...] += _dot(hid.astype(BF16), wd_ref[0].astype(BF16))

    @pl.when(e == n_exp - 1)
    def _():
        o_ref[...] = _layer_norm(DEEPNORM_ALPHA * h_ref[...] + acc_ref[...], g_ref[...], b_ref[...])


def _moe(h, gates, wg, wu, wd, sg, su, sd, g, b):
    n, d = h.shape
    n_exp, _, f = wg.shape
    tm = min(1024, n)
    full = lambda a: pl.BlockSpec(a.shape, lambda i, e: (0,) * a.ndim)
    return pl.pallas_call(
        functools.partial(_moe_body, n_exp=n_exp),
        grid=(n // tm, n_exp),
        in_specs=[pl.BlockSpec((tm, d), lambda i, e: (i, 0)), pl.BlockSpec((tm, n_exp), lambda i, e: (i, 0)),
                  pl.BlockSpec((1, d, f), lambda i, e: (e, 0, 0)), pl.BlockSpec((1, d, f), lambda i, e: (e, 0, 0)),
                  pl.BlockSpec((1, f, d), lambda i, e: (e, 0, 0)),
                  full(sg), full(su), full(sd), full(g), full(b)],
        out_specs=pl.BlockSpec((tm, d), lambda i, e: (i, 0)),
        out_shape=jax.ShapeDtypeStruct((n, d), F32),
        scratch_shapes=[pltpu.VMEM((tm, d), F32)],
        compiler_params=_cparams(("parallel", "arbitrary")),
        name="moe",
    )(h, gates, wg, wu, wd, sg, su, sd, g, b)


def _head_block_diag(q):
    lane_head = lax.broadcasted_iota(jnp.int32, q.shape, 1) // HEAD_DIM
    return jnp.concatenate([jnp.where(lane_head == h, q, 0.0) for h in range(MOBA_HEADS)], axis=0)


def _sample_keys_body(pt_ref, q_ref, *refs, ppstep):
    pages, sc_ref = refs[:ppstep], refs[ppstep]
    qbd = (_head_block_diag(q_ref[0]) * SCALE).astype(BF16)
    for i in range(ppstep):
        kt = pages[i][0].reshape(MOBA_W, PAGE_SIZE)
        sc_ref[0, :, i * PAGE_SIZE:(i + 1) * PAGE_SIZE] = _dot(qbd, kt.astype(BF16))


def _page_spec(n_pages, ppstep, i):
    return pl.BlockSpec((1, MOBA_HEADS, HEAD_DIM, PAGE_SIZE),
                        lambda b, s, pt: (pt[b * n_pages + s * ppstep + i], 0, 0, 0))


def _sample_keys(pt_flat, q3, ckT, n_pages):
    b, t, w = q3.shape
    ppstep = min(SAMPLE_PAGES_PER_STEP, n_pages)
    ht = MOBA_HEADS * t
    grid_spec = pltpu.PrefetchScalarGridSpec(
        num_scalar_prefetch=1,
        grid=(b, n_pages // ppstep),
        in_specs=[pl.BlockSpec((1, t, w), lambda b_, s, pt: (b_, 0, 0))]
        + [_page_spec(n_pages, ppstep, i) for i in range(ppstep)],
        out_specs=pl.BlockSpec((1, ht, ppstep * PAGE_SIZE), lambda b_, s, pt: (b_, 0, s)),
    )
    return pl.pallas_call(
        functools.partial(_sample_keys_body, ppstep=ppstep),
        grid_spec=grid_spec,
        out_shape=jax.ShapeDtypeStruct((b, ht, n_pages * PAGE_SIZE), F32),
        compiler_params=_cparams(("arbitrary", "arbitrary")),
        name="moba_sample_keys",
    )(pt_flat, q3, *([ckT] * ppstep))


def _sample_values_body(pt_ref, sc_ref, q_ref, kn_ref, vn_ref, *refs, ppstep, n_pages, t_new):
    pages, o_ref, (lg_s, p_s, l_s, acc_s) = refs[:ppstep], refs[ppstep], refs[ppstep + 1:]
    s = pl.program_id(1)
    ppb = MOBA_BLOCK // PAGE_SIZE
    n_fp = n_pages // ppb
    n_sel = min(MOBA_TOPK, n_fp)
    past = n_pages * PAGE_SIZE
    ht = MOBA_HEADS * t_new
    bps = ppstep // ppb

    @pl.when(s == 0)
    def _():
        qbd = _head_block_diag(q_ref[0])
        blane = lax.broadcasted_iota(jnp.int32, (ht, LANES), 1)
        gate = jnp.full((ht, LANES), -jnp.inf, F32)
        for n in range(n_fp):
            gsum = jnp.sum(sc_ref[0, :, n * MOBA_BLOCK:(n + 1) * MOBA_BLOCK], axis=1, keepdims=True)
            gate = jnp.where(blane == n, gsum, gate)
        sel = _topk_mask(gate, n_sel, 1)

        row = lax.broadcasted_iota(jnp.int32, (ht, 1), 0)
        slope = jnp.exp2(-((row // t_new) + 1).astype(F32))
        t_col = row % t_new
        qs = (qbd * SCALE).astype(BF16)
        pad = jnp.zeros((LANES - t_new, MOBA_W), F32)
        knew = jnp.concatenate([kn_ref[0], pad], axis=0).astype(BF16)
        vnew = jnp.concatenate([vn_ref[0], pad], axis=0).astype(BF16)
        l_own = _dot_nt(qs, knew) - slope * (t_col - blane).astype(F32)
        l_own = jnp.where(blane <= t_col, l_own, NEG)
        pos = lax.broadcasted_iota(jnp.int32, (ht, MOBA_BLOCK), 1)
        base = slope * (t_col - pos).astype(F32)
        m_el = jnp.full((ht, LANES), NEG, F32)
        for n in range(n_fp):
            cols = slice(n * MOBA_BLOCK, (n + 1) * MOBA_BLOCK)
            ln = sc_ref[0, :, cols] - base - slope * float(past - n * MOBA_BLOCK)
            ln = jnp.where(sel[:, n:n + 1] > 0.5, ln, NEG)
            m_el = jnp.maximum(m_el, jnp.maximum(ln[:, :LANES], ln[:, LANES:]))
            lg_s[:, cols] = ln
        m = jnp.maximum(jnp.max(l_own, axis=1, keepdims=True), jnp.max(m_el, axis=1, keepdims=True))
        p_own = jnp.exp(l_own - m)
        s_el = jnp.zeros((ht, LANES), F32)
        for n in range(n_fp):
            pn = jnp.exp(lg_s[:, n * MOBA_BLOCK:(n + 1) * MOBA_BLOCK] - m)
            s_el = s_el + (pn[:, :LANES] + pn[:, LANES:])
            p_s[n // bps, :, (n % bps) * MOBA_BLOCK:(n % bps + 1) * MOBA_BLOCK] = pn.astype(BF16)
        lsum = jnp.sum(p_own, axis=1, keepdims=True) + jnp.sum(s_el, axis=1, keepdims=True)
        l_s[...] = jnp.broadcast_to(lsum, (ht, LANES))
        acc_s[...] = _dot(p_own.astype(BF16), vnew)

    vt = jnp.concatenate([pages[i][0].reshape(MOBA_W, PAGE_SIZE).astype(BF16) for i in range(ppstep)], axis=1)
    acc_s[...] += _dot_nt(p_s[s], vt)

    @pl.when(s == pl.num_programs(1) - 1)
    def _():
        o_all = acc_s[...] / l_s[:, 0:1]
        lane_head = lax.broadcasted_iota(jnp.int32, (t_new, MOBA_W), 1) // HEAD_DIM
        out = jnp.zeros((t_new, MOBA_W), F32)
        for h in range(MOBA_HEADS):
            out = out + jnp.where(lane_head == h, o_all[h * t_new:(h + 1) * t_new], 0.0)
        o_ref[0] = out


def _sample_values(pt_flat, scores, q3, kn3, vn3, cvT, n_pages):
    b, t, w = q3.shape
    ppstep = min(SAMPLE_PAGES_PER_STEP, n_pages)
    ht = MOBA_HEADS * t
    n_steps = n_pages // ppstep
    small = lambda a: pl.BlockSpec((1,) + a.shape[1:], lambda b_, s, pt: (b_,) + (0,) * (a.ndim - 1))
    grid_spec = pltpu.PrefetchScalarGridSpec(
        num_scalar_prefetch=1,
        grid=(b, n_steps),
        in_specs=[small(scores), small(q3), small(kn3), small(vn3)]
        + [_page_spec(n_pages, ppstep, i) for i in range(ppstep)],
        out_specs=pl.BlockSpec((1, t, w), lambda b_, s, pt: (b_, 0, 0)),
        scratch_shapes=[pltpu.VMEM((ht, n_pages * PAGE_SIZE), F32),
                        pltpu.VMEM((n_steps, ht, ppstep * PAGE_SIZE), BF16),
                        pltpu.VMEM((ht, LANES), F32),
                        pltpu.VMEM((ht, w), F32)],
    )
    return pl.pallas_call(
        functools.partial(_sample_values_body, ppstep=ppstep, n_pages=n_pages, t_new=t),
        grid_spec=grid_spec,
        out_shape=jax.ShapeDtypeStruct((b, t, w), F32),
        compiler_params=_cparams(("arbitrary", "arbitrary")),
        name="moba_sample_values",
    )(pt_flat, scores, q3, kn3, vn3, *([cvT] * ppstep))


def _sgu_block_diag(w_s, t_new):
    g = w_s.shape[0]
    reps = SGU_CHUNK // t_new
    small = jnp.tril(w_s)[:, :t_new, :t_new]
    eye = jnp.eye(reps, dtype=w_s.dtype)
    return jnp.einsum("rq,gts->grtqs", eye, small).reshape(g, SGU_CHUNK, SGU_CHUNK)


def _sgu_bias(b_s, rows):
    return jnp.repeat(b_s[:, :rows].T, HEAD_DIM, axis=1)


def _ffn(h1, wrT, rb_col, moe_w, g, b):
    gates = _router(h1, wrT, rb_col).T
    return _moe(h1, gates, *moe_w, g, b)


def kernel(x_prompt, x_sample, mem_prompt, cache_k, cache_v, page_table, cache_mem_k, cache_mem_v, w_in, w_out, w_mem_kv, sgu_ln_g, sgu_ln_b, sgu_w_s, sgu_b_s, ln1_g, ln1_b, w_router, router_bias, w_gate, w_up, w_down, ws_gate, ws_up, ws_down, ln2_g, ln2_b):
    assert w_in.shape[0] == DEPTH and x_prompt.shape[0] == 1
    _, seq, d_model = x_prompt.shape
    db, t_new, _ = x_sample.shape
    n_pages = page_table.shape[1]
    assert n_pages % (MOBA_BLOCK // PAGE_SIZE) == 0 and SGU_CHUNK % t_new == 0
    n_s = db * t_new
    assert n_s % SGU_CHUNK == 0 and seq % (8 * MOBA_BLOCK) == 0 and seq % (ATT_GROUP * MOBA_BLOCK) == 0
    assert n_pages % min(SAMPLE_PAGES_PER_STEP, n_pages) == 0

    w_in_bf = w_in[0].astype(BF16)
    wo_bf = w_out[0].astype(BF16)
    wmem_bf = w_mem_kv[0].astype(BF16)
    sg_g, sg_b = sgu_ln_g, sgu_ln_b
    moe_w = (w_gate[0], w_up[0], w_down[0],
             ws_gate[0].astype(BF16), ws_up[0].astype(BF16), ws_down[0].astype(BF16))
    wrT = jnp.swapaxes(w_router[0], 0, 1)
    rb_col = router_bias[0][:, None]

    xp = x_prompt[0]
    q, k, v, u, svn, qm = _in_proj(xp, w_in_bf, sg_g, sg_b)
    mkv = _mem_kv(mem_prompt[0], wmem_bf)
    nblk = seq // MOBA_BLOCK
    kmean = _k_block_mean(k)
    gk = ATT_GROUP * MOBA_BLOCK
    ngrp = seq // gk
    vT = v.astype(BF16).reshape(ngrp, gk, N_PAIRS, ATT_PAIR_W).transpose(0, 2, 3, 1)
    vTg = jnp.concatenate([vT, jnp.ones((ngrp, N_PAIRS, ATT_ONES, gk), BF16)], axis=2)
    vTg = vTg.reshape(ngrp, N_PAIRS * ATT_SLAB, gk)
    o_a = _moba_prompt(q.T, k.astype(BF16), vTg, kmean).T
    o_c = _mem_attend_prompt(qm, mkv)
    wm_p = jnp.tril(sgu_w_s[0]).astype(BF16)
    h1 = _mix(xp, o_a, u, svn, o_c, wm_p, _sgu_bias(sgu_b_s[0], SGU_CHUNK), wo_bf, ln1_g, ln1_b)
    y_prompt = _ffn(h1, wrT, rb_col, moe_w, ln2_g, ln2_b)[None]

    n_pg = seq // PAGE_SIZE
    to_pages = lambda z: z.reshape(1, 1, n_pg, PAGE_SIZE, MOBA_HEADS, HEAD_DIM).transpose(0, 1, 2, 4, 3, 5)
    k_prompt, v_prompt = to_pages(k), to_pages(v)
    n_mem = mkv.shape[0]
    mem_k_prompt = mkv[:, :MEM_W].reshape(1, 1, n_mem, MEM_HEADS, HEAD_DIM)
    mem_v_prompt = mkv[:, MEM_W:].reshape(1, 1, n_mem, MEM_HEADS, HEAD_DIM)

    xs = x_sample.reshape(n_s, d_model)
    q, k, v, u, svn, qm = _in_proj(xs, w_in_bf, sg_g, sg_b)
    heads = lambda z: z.reshape(db, t_new, MOBA_HEADS, HEAD_DIM).transpose(0, 2, 1, 3)
    k4, v4 = heads(k), heads(v)
    ckT = jnp.swapaxes(cache_k[0], -1, -2)
    cvT = jnp.swapaxes(cache_v[0], -1, -2)
    pt_flat = page_table.reshape(-1)
    three = lambda z: z.reshape(db, t_new, MOBA_W)
    scores = _sample_keys(pt_flat, three(q), ckT, n_pages)
    o_a = _sample_values(pt_flat, scores, three(q), three(k), three(v), cvT, n_pages).reshape(n_s, MOBA_W)
    mkT = cache_mem_k[0].transpose(0, 2, 3, 1)
    mvT = cache_mem_v[0].transpose(0, 2, 3, 1)
    o_c = _mem_attend_sample(qm.reshape(db, t_new, MEM_W), mkT, mvT).reshape(n_s, MEM_W)
    wm_s = _sgu_block_diag(sgu_w_s[0], t_new).astype(BF16)
    bz_s = jnp.tile(_sgu_bias(sgu_b_s[0], t_new), (SGU_CHUNK // t_new, 1))
    h1 = _mix(xs, o_a, u, svn, o_c, wm_s, bz_s, wo_bf, ln1_g, ln1_b)
    y_sample = _ffn(h1, wrT, rb_col, moe_w, ln2_g, ln2_b).reshape(db, t_new, d_model)

    k_sample, v_sample = k4[None], v4[None]
    sgu_v_sample = svn.reshape(1, db, t_new, SGU_W)
    return (y_prompt, y_sample, k_prompt, v_prompt, k_sample, v_sample, mem_k_prompt, mem_v_prompt, sgu_v_sample)
```

```python
import functools

import jax
import jax.numpy as jnp
from jax import lax
from jax.experimental import pallas as pl
from jax.experimental.pallas import tpu as pltpu

F32 = jnp.float32
BF16 = jnp.bfloat16

HEAD_DIM = 64
MOBA_HEADS = 8
SGU_GROUPS = 4
MEM_HEADS = 4
MOBA_W = MOBA_HEADS * HEAD_DIM
SGU_W = SGU_GROUPS * HEAD_DIM
MEM_W = MEM_HEADS * HEAD_DIM
MOBA_BLOCK = 256
MOBA_TOPK = 3
Q_BLOCK = 128
SGU_CHUNK = 128
PAGE_SIZE = 128
N_EXPERT_GROUPS = 8
TOPK_GROUPS = 4
TOP_K = 8
ROUTE_SCALE = 2.5
LN_EPS = 1e-5
NEG = -1e30
DEPTH = 1
DEEPNORM_ALPHA = (2 * DEPTH) ** 0.25
SCALE = HEAD_DIM ** -0.5
LOG2E = 1.4426950408889634
ATT_GROUP = 8
SAMPLE_PAGES_PER_STEP = 64
MOE_EXPERTS_PER_STEP = 4
LANES = 128
VMEM_LIMIT = 48 * 1024 * 1024
LARGE_VMEM_LIMIT = 58 * 1024 * 1024
HIGHEST = lax.Precision.HIGHEST


def _cparams(sem, limit=VMEM_LIMIT):
    return pltpu.CompilerParams(dimension_semantics=sem, vmem_limit_bytes=limit)


def _gelu(x):
    return x * (0.5 * (1.0 + jnp.tanh(0.7978845608028654 * (x + 0.044715 * (x * x * x)))))


def _sigmoid(x):
    return 1.0 / (1.0 + jnp.exp(-x))


def _silu(x):
    return x * _sigmoid(x)


def _layer_norm(x, g, b):
    mu = jnp.mean(x, axis=-1, keepdims=True)
    xc = x - mu
    var = jnp.mean(xc * xc, axis=-1, keepdims=True)
    return xc * lax.rsqrt(var + LN_EPS) * g + b


def _topk_mask(x, k, axis):
    n = x.shape[axis]
    idx = lax.broadcasted_iota(jnp.int32, x.shape, axis).astype(F32)
    sel = jnp.zeros(x.shape, F32)
    for _ in range(k):
        mx = jnp.max(x, axis=axis, keepdims=True)
        first = jnp.min(jnp.where(x == mx, idx, float(n)), axis=axis, keepdims=True)
        hit = idx == first
        sel = jnp.where(hit, 1.0, sel)
        x = jnp.where(hit, -jnp.inf, x)
    return sel


def _dot(a, b):
    return jnp.dot(a, b, preferred_element_type=F32)


def _dot_nt(a, b, precision=None):
    return lax.dot_general(a, b, (((1,), (1,)), ((), ())), precision=precision, preferred_element_type=F32)


def _project(x_ref, w_ref, g_ref, b_ref):
    x = x_ref[...].astype(BF16)
    c0, c1, c2, c3, c4 = MOBA_W, 2 * MOBA_W, 3 * MOBA_W, 3 * MOBA_W + SGU_W, 3 * MOBA_W + 2 * SGU_W
    q = _dot(x, w_ref[:, 0:c0])
    k = _dot(x, w_ref[:, c0:c1])
    v = _dot(x, w_ref[:, c1:c2])
    u = _gelu(_dot(x, w_ref[:, c2:c3]))
    sv = _layer_norm(_gelu(_dot(x, w_ref[:, c3:c4])), g_ref[...], b_ref[...])
    qm = _dot(x, w_ref[:, c4:c4 + MEM_W])
    return q, k, v, u, sv, qm


def _inproj_body(x_ref, w_ref, g_ref, b_ref, *out_refs):
    for ref, val in zip(out_refs, _project(x_ref, w_ref, g_ref, b_ref), strict=True):
        ref[...] = val


def _inproj_prompt_body(x_ref, w_ref, g_ref, b_ref, qT_ref, k_ref, kbf_ref, kTp_ref, vTp_ref, vTg_ref,
                        u_ref, sv_ref, qm_ref):
    q, k, v, u, sv, qm = _project(x_ref, w_ref, g_ref, b_ref)
    tm = q.shape[0]
    qT_ref[...] = q.T
    k_ref[...] = k
    kbf_ref[...] = k.astype(BF16)
    kT, vT = k.T, v.T
    for p in range(tm // PAGE_SIZE):
        kTp_ref[p] = kT[:, p * PAGE_SIZE:(p + 1) * PAGE_SIZE]
        vTp_ref[p] = vT[:, p * PAGE_SIZE:(p + 1) * PAGE_SIZE]
    vT_bf = vT.astype(BF16)
    ones = jnp.ones((ATT_ONES, tm), BF16)
    for c in range(N_PAIRS):
        vTg_ref[0, c * ATT_SLAB:c * ATT_SLAB + ATT_PAIR_W, :] = vT_bf[c * ATT_PAIR_W:(c + 1) * ATT_PAIR_W]
        vTg_ref[0, c * ATT_SLAB + ATT_PAIR_W:(c + 1) * ATT_SLAB, :] = ones
    u_ref[...] = u
    sv_ref[...] = sv
    qm_ref[...] = qm


def _in_proj_prompt(x, w_in_bf, sgu_g, sgu_b):
    n, d = x.shape
    tm = 512
    gk = ATT_GROUP * MOBA_BLOCK
    per_grp = gk // tm
    ppt = tm // PAGE_SIZE
    row = lambda w: pl.BlockSpec((tm, w), lambda i: (i, 0))
    full = lambda a: pl.BlockSpec(a.shape, lambda i: (0,) * a.ndim)
    pages = pl.BlockSpec((ppt, MOBA_W, PAGE_SIZE), lambda i: (i, 0, 0))
    sds = jax.ShapeDtypeStruct
    return pl.pallas_call(
        _inproj_prompt_body,
        grid=(n // tm,),
        in_specs=[row(d), full(w_in_bf), full(sgu_g), full(sgu_b)],
        out_specs=[pl.BlockSpec((MOBA_W, tm), lambda i: (0, i)), row(MOBA_W), row(MOBA_W), pages, pages,
                   pl.BlockSpec((1, N_PAIRS * ATT_SLAB, tm), lambda i: (i // per_grp, 0, i % per_grp)),
                   row(SGU_W), row(SGU_W), row(MEM_W)],
        out_shape=[sds((MOBA_W, n), F32), sds((n, MOBA_W), F32), sds((n, MOBA_W), BF16),
                   sds((n // PAGE_SIZE, MOBA_W, PAGE_SIZE), F32), sds((n // PAGE_SIZE, MOBA_W, PAGE_SIZE), F32),
                   sds((n // gk, N_PAIRS * ATT_SLAB, gk), BF16),
                   sds((n, SGU_W), F32), sds((n, SGU_W), F32), sds((n, MEM_W), F32)],
        compiler_params=_cparams(("parallel",)),
        name="in_proj_prompt",
    )(x, w_in_bf, sgu_g, sgu_b)


def _in_proj(x, w_in_bf, sgu_g, sgu_b):
    n, d = x.shape
    tm = min(512, n)
    in_w = w_in_bf.shape[1]
    row = lambda w: pl.BlockSpec((tm, w), lambda i: (i, 0))
    full = lambda a: pl.BlockSpec(a.shape, lambda i: (0,) * a.ndim)
    widths = (MOBA_W, MOBA_W, MOBA_W, SGU_W, SGU_W, MEM_W)
    return pl.pallas_call(
        _inproj_body,
        grid=(n // tm,),
        in_specs=[row(d), full(w_in_bf), full(sgu_g), full(sgu_b)],
        out_specs=[row(w) for w in widths],
        out_shape=[jax.ShapeDtypeStruct((n, w), F32) for w in widths],
        compiler_params=_cparams(("parallel",)),
        name="in_proj",
    )(x, w_in_bf, sgu_g, sgu_b)


def _memkv_body(m_ref, w_ref, o_ref):
    o_ref[...] = _dot(m_ref[...].astype(BF16), w_ref[...])


def _mem_kv(mem, w_bf):
    n, _ = mem.shape
    return pl.pallas_call(
        _memkv_body,
        out_shape=jax.ShapeDtypeStruct((n, w_bf.shape[1]), F32),
        compiler_params=pltpu.CompilerParams(vmem_limit_bytes=VMEM_LIMIT),
        name="mem_kv",
    )(mem, w_bf)


def _kmean_body(k_ref, o_ref):
    kb = k_ref[...].reshape(8, MOBA_BLOCK, k_ref.shape[1])
    o_ref[...] = jnp.sum(kb, axis=1) * (1.0 / MOBA_BLOCK)


def _k_block_mean(k):
    s, w = k.shape
    nblk = s // MOBA_BLOCK
    return pl.pallas_call(
        _kmean_body,
        grid=(nblk // 8,),
        in_specs=[pl.BlockSpec((8 * MOBA_BLOCK, w), lambda i: (i, 0))],
        out_specs=pl.BlockSpec((8, w), lambda i: (i, 0)),
        out_shape=jax.ShapeDtypeStruct((nblk, w), F32),
        compiler_params=_cparams(("parallel",)),
        name="k_block_mean",
    )(k)


ATT_PAIR_W = 2 * HEAD_DIM
ATT_ONES = 16
ATT_SLAB = ATT_PAIR_W + ATT_ONES
N_PAIRS = MOBA_HEADS // 2


def _moba_prompt_body(qT_ref, k_ref, vTg_ref, km_ref, oT_ref, bias_s, dmat_s, qs_s, *, nblk):
    qi = pl.program_id(0)
    own = qi // (MOBA_BLOCK // Q_BLOCK)
    half = qi % (MOBA_BLOCK // Q_BLOCK)
    w2 = 2 * Q_BLOCK
    gk = ATT_GROUP * MOBA_BLOCK

    lane = lax.broadcasted_iota(jnp.int32, (1, w2), 1)
    tl = lane % Q_BLOCK
    sl = lax.broadcasted_iota(jnp.int32, (MOBA_BLOCK, w2), 0)
    n_iota = lax.broadcasted_iota(jnp.int32, (nblk, w2), 0)
    blk_off = (n_iota * MOBA_BLOCK - qi * Q_BLOCK).astype(F32)
    causal = half * Q_BLOCK + tl - sl >= 0
    z = jnp.zeros((HEAD_DIM, Q_BLOCK), F32)

    state = []
    for c in range(N_PAIRS):
        head = 2 * c + (lane >= Q_BLOCK).astype(jnp.int32)
        slope = jnp.exp2(-(head + 1).astype(F32)) * LOG2E

        @pl.when(qi == 0)
        def _():
            dmat_s[c] = slope * sl.astype(F32)

        qT = qT_ref[c * ATT_PAIR_W:(c + 1) * ATT_PAIR_W, :]
        qbd = jnp.concatenate(
            [jnp.concatenate([qT[:HEAD_DIM], z], axis=1), jnp.concatenate([z, qT[HEAD_DIM:]], axis=1)], axis=0)
        qs_s[c] = (qbd * (SCALE * LOG2E)).astype(BF16)

        gate = jnp.dot(km_ref[:, c * ATT_PAIR_W:(c + 1) * ATT_PAIR_W], qbd, precision=HIGHEST,
                       preferred_element_type=F32)
        gate = jnp.where(n_iota < own, gate, -jnp.inf)
        sel = jnp.logical_and(_topk_mask(gate, MOBA_TOPK, 0) > 0.5, n_iota < own)
        bias_s[c] = jnp.where(sel, slope * blk_off, NEG)

        k_own = k_ref[pl.ds(pl.multiple_of(own * MOBA_BLOCK, MOBA_BLOCK), MOBA_BLOCK),
                      c * ATT_PAIR_W:(c + 1) * ATT_PAIR_W]
        s0 = _dot(k_own, qs_s[c]) + dmat_s[c] - slope * (half * Q_BLOCK).astype(F32)
        s0 = jnp.where(causal, s0, NEG)
        m0 = jnp.max(s0, axis=0, keepdims=True)
        p0 = jnp.exp2(s0 - m0).astype(BF16)
        col0 = pl.multiple_of((own % ATT_GROUP) * MOBA_BLOCK, MOBA_BLOCK)
        res0 = _dot(vTg_ref[own // ATT_GROUP, c * ATT_SLAB:(c + 1) * ATT_SLAB, pl.ds(col0, MOBA_BLOCK)], p0)
        state.append((m0, res0[ATT_PAIR_W:ATT_PAIR_W + 1], res0[:ATT_PAIR_W]))

    def body(j, carry):
        out = []
        for c in range(N_PAIRS):
            m, l, acc = carry[c]
            parts = []
            m_new = m
            for g in range(ATT_GROUP):
                row0 = pl.multiple_of((j * ATT_GROUP + g) * MOBA_BLOCK, MOBA_BLOCK)
                sg = _dot(k_ref[pl.ds(row0, MOBA_BLOCK), c * ATT_PAIR_W:(c + 1) * ATT_PAIR_W], qs_s[c])
                sg = sg + dmat_s[c] + bias_s[c, pl.ds(j * ATT_GROUP + g, 1), :]
                m_new = jnp.maximum(m_new, jnp.max(sg, axis=0, keepdims=True))
                parts.append(sg)
            a = jnp.exp2(m - m_new)
            res = None
            for g in range(ATT_GROUP):
                pg = jnp.exp2(parts[g] - m_new).astype(BF16)
                vt = vTg_ref[j, c * ATT_SLAB:(c + 1) * ATT_SLAB, g * MOBA_BLOCK:(g + 1) * MOBA_BLOCK]
                res = _dot(vt, pg) if res is None else res + _dot(vt, pg)
            out.append((m_new, a * l + res[ATT_PAIR_W:ATT_PAIR_W + 1], a * acc + res[:ATT_PAIR_W]))
        return tuple(out)

    n_trip = (own + (ATT_GROUP - 1)) // ATT_GROUP
    state = lax.fori_loop(0, n_trip, body, tuple(state))
    for c in range(N_PAIRS):
        _, l, acc = state[c]
        inv = 1.0 / l
        r0 = c * ATT_PAIR_W
        oT_ref[r0:r0 + HEAD_DIM, :] = acc[0:HEAD_DIM, 0:Q_BLOCK] * inv[:, 0:Q_BLOCK]
        oT_ref[r0 + HEAD_DIM:r0 + ATT_PAIR_W, :] = acc[HEAD_DIM:, Q_BLOCK:] * inv[:, Q_BLOCK:]


def _moba_prompt(qT, k_bf, vTg, kmean):
    w, s = qT.shape
    nblk = kmean.shape[0]
    resident = lambda a: pl.BlockSpec(a.shape, lambda i: (0,) * a.ndim, pipeline_mode=pl.Buffered(1))
    return pl.pallas_call(
        functools.partial(_moba_prompt_body, nblk=nblk),
        grid=(s // Q_BLOCK,),
        in_specs=[pl.BlockSpec((w, Q_BLOCK), lambda i: (0, i)), resident(k_bf), resident(vTg), resident(kmean)],
        out_specs=pl.BlockSpec((w, Q_BLOCK), lambda i: (0, i)),
        out_shape=jax.ShapeDtypeStruct((w, s), F32),
        scratch_shapes=[pltpu.VMEM((N_PAIRS, nblk, 2 * Q_BLOCK), F32),
                        pltpu.VMEM((N_PAIRS, MOBA_BLOCK, 2 * Q_BLOCK), F32),
                        pltpu.VMEM((N_PAIRS, ATT_PAIR_W, 2 * Q_BLOCK), BF16)],
        compiler_params=_cparams(("arbitrary",), LARGE_VMEM_LIMIT),
        name="moba_prompt",
    )(qT, k_bf, vTg, kmean)


def _softmax_rows(logits):
    m = jnp.max(logits, axis=-1, keepdims=True)
    e = jnp.exp(logits - m)
    return e / jnp.sum(e, axis=-1, keepdims=True)


def _mem_prompt_body(qm_ref, mkv_ref, o_ref):
    outs = []
    for h in range(MEM_HEADS):
        lo, hi = h * HEAD_DIM, (h + 1) * HEAD_DIM
        q = (qm_ref[:, lo:hi] * SCALE).astype(BF16)
        mk = mkv_ref[:, lo:hi].astype(BF16)
        mv = mkv_ref[:, MEM_W + lo:MEM_W + hi].astype(BF16)
        p = _softmax_rows(_dot_nt(q, mk))
        outs.append(_dot(p.astype(BF16), mv))
    o_ref[...] = jnp.concatenate(outs, axis=1)


def _mem_attend_prompt(qm, mkv):
    n, w = qm.shape
    tm = min(512, n)
    return pl.pallas_call(
        _mem_prompt_body,
        grid=(n // tm,),
        in_specs=[pl.BlockSpec((tm, w), lambda i: (i, 0)), pl.BlockSpec(mkv.shape, lambda i: (0, 0))],
        out_specs=pl.BlockSpec((tm, w), lambda i: (i, 0)),
        out_shape=jax.ShapeDtypeStruct((n, w), F32),
        compiler_params=_cparams(("parallel",)),
        name="mem_attend_prompt",
    )(qm, mkv)


def _mem_sample_body(qm_ref, mkT_ref, mvT_ref, o_ref, *, sb):
    for i in range(sb):
        outs = []
        for h in range(MEM_HEADS):
            lo, hi = h * HEAD_DIM, (h + 1) * HEAD_DIM
            q = (qm_ref[i, :, lo:hi] * SCALE).astype(BF16)
            p = _softmax_rows(_dot(q, mkT_ref[i, h].astype(BF16)))
            outs.append(_dot_nt(p.astype(BF16), mvT_ref[i, h].astype(BF16)))
        o_ref[i] = jnp.concatenate(outs, axis=1)


def _mem_attend_sample(qm3, mkT, mvT):
    b, t, w = qm3.shape
    sb = min(8, b)
    blk = (sb,) + mkT.shape[1:]
    return pl.pallas_call(
        functools.partial(_mem_sample_body, sb=sb),
        grid=(b // sb,),
        in_specs=[pl.BlockSpec((sb, t, w), lambda i: (i, 0, 0)),
                  pl.BlockSpec(blk, lambda i: (i, 0, 0, 0)), pl.BlockSpec(blk, lambda i: (i, 0, 0, 0))],
        out_specs=pl.BlockSpec((sb, t, w), lambda i: (i, 0, 0)),
        out_shape=jax.ShapeDtypeStruct((b, t, w), F32),
        compiler_params=_cparams(("parallel",)),
        name="mem_attend_sample",
    )(qm3, mkT, mvT)


def _mix_body(x_ref, oa_ref, u_ref, vn_ref, oc_ref, wm_ref, bz_ref, wo_ref, g_ref, b_ref, h_ref, *, tm, oa_transposed):
    lane_grp = lax.broadcasted_iota(jnp.int32, (SGU_CHUNK, SGU_W), 1) // HEAD_DIM
    zs = []
    for c in range(tm // SGU_CHUNK):
        vn = vn_ref[c * SGU_CHUNK:(c + 1) * SGU_CHUNK, :].astype(BF16)
        z = bz_ref[...]
        for g in range(SGU_GROUPS):
            z = z + jnp.where(lane_grp == g, _dot(wm_ref[g], vn), 0.0)
        zs.append(z)
    ob = u_ref[...] * jnp.concatenate(zs, axis=0)
    oa = oa_ref[...].T if oa_transposed else oa_ref[...]
    y = _dot(oa.astype(BF16), wo_ref[0:MOBA_W, :])
    y = y + _dot(ob.astype(BF16), wo_ref[MOBA_W:MOBA_W + SGU_W, :])
    y = y + _dot(oc_ref[...].astype(BF16), wo_ref[MOBA_W + SGU_W:, :])
    h_ref[...] = _layer_norm(DEEPNORM_ALPHA * x_ref[...] + y, g_ref[...], b_ref[...])


def _mix(x, oa, u, vn, oc, wmat_bf, bz, wo_bf, g, b, oa_transposed=False):
    n, d = x.shape
    tm = min(512, n)
    row = lambda a: pl.BlockSpec((tm, a.shape[1]), lambda i: (i, 0))
    full = lambda a: pl.BlockSpec(a.shape, lambda i: (0,) * a.ndim)
    oa_spec = pl.BlockSpec((oa.shape[0], tm), lambda i: (0, i)) if oa_transposed else row(oa)
    return pl.pallas_call(
        functools.partial(_mix_body, tm=tm, oa_transposed=oa_transposed),
        grid=(n // tm,),
        in_specs=[row(x), oa_spec, row(u), row(vn), row(oc), full(wmat_bf), full(bz), full(wo_bf), full(g), full(b)],
        out_specs=row(x),
        out_shape=jax.ShapeDtypeStruct((n, d), F32),
        compiler_params=_cparams(("parallel",)),
        name="mix",
    )(x, oa, u, vn, oc, wmat_bf, bz, wo_bf, g, b)


def _rank_rows(x, n_rows, axis):
    idx = lax.broadcasted_iota(jnp.int32, x.shape, axis)
    rank = jnp.zeros(x.shape, F32)
    for j in range(n_rows):
        xj = lax.slice_in_dim(x, j, j + 1, axis=axis)
        tie = jnp.where(idx > j, 1.0, 0.0)
        rank = rank + jnp.where(xj > x, 1.0, jnp.where(xj == x, tie, 0.0))
    return rank


def _router_body(h_ref, wrT_ref, rb_ref, o_ref, *, n_exp):
    tm = h_ref.shape[0]
    gsz = n_exp // N_EXPERT_GROUPS
    scores = _sigmoid(_dot_nt(wrT_ref[...], h_ref[...], precision=HIGHEST))
    biased = scores + rb_ref[...]
    b3 = biased.reshape(N_EXPERT_GROUPS, gsz, tm)
    top2 = jnp.sum(jnp.where(_rank_rows(b3, gsz, 1) < 2.0, b3, 0.0), axis=1, keepdims=True)
    gs = jnp.broadcast_to(top2, b3.shape).reshape(n_exp, tm)
    grp = lax.broadcasted_iota(jnp.int32, (n_exp, tm), 0) // gsz
    grank = jnp.zeros((n_exp, tm), F32)
    for g2 in range(N_EXPERT_GROUPS):
        xg = gs[g2 * gsz:g2 * gsz + 1, :]
        tie = jnp.where(grp > g2, 1.0, 0.0)
        grank = grank + jnp.where(xg > gs, 1.0, jnp.where(xg == gs, tie, 0.0))
    masked = jnp.where(grank < float(TOPK_GROUPS), biased, -jnp.inf)
    w = _topk_mask(masked, TOP_K, 0) * scores
    o_ref[...] = w / jnp.sum(w, axis=0, keepdims=True) * ROUTE_SCALE


def _router(h, wrT, rb_col):
    n, d = h.shape
    n_exp = wrT.shape[0]
    tm = min(512, n)
    return pl.pallas_call(
        functools.partial(_router_body, n_exp=n_exp),
        grid=(n // tm,),
        in_specs=[pl.BlockSpec((tm, d), lambda i: (i, 0)), pl.BlockSpec(wrT.shape, lambda i: (0, 0)),
                  pl.BlockSpec(rb_col.shape, lambda i: (0, 0))],
        out_specs=pl.BlockSpec((n_exp, tm), lambda i: (0, i)),
        out_shape=jax.ShapeDtypeStruct((n_exp, n), F32),
        compiler_params=_cparams(("parallel",)),
        name="router",
    )(h, wrT, rb_col)


def _moe_body(h_ref, gt_ref, wg_ref, wu_ref, wd_ref, sg_ref, su_ref, sd_ref, g_ref, b_ref, o_ref, acc_ref, *, eps):
    s = pl.program_id(1)
    hb = h_ref[...].astype(BF16)

    @pl.when(s == 0)
    def _():
        hs = _silu(_dot(hb, sg_ref[...])) * _dot(hb, su_ref[...])
        acc_ref[...] = _dot(hs.astype(BF16), sd_ref[...])

    lane = lax.broadcasted_iota(jnp.int32, gt_ref.shape, 1)
    hids = []
    for j in range(eps):
        gcol = jnp.sum(jnp.where(lane == s * eps + j, gt_ref[...], 0.0), axis=1, keepdims=True)
        hid = _silu(_dot(hb, wg_ref[j].astype(BF16))) * _dot(hb, wu_ref[j].astype(BF16)) * gcol
        hids.append(hid.astype(BF16))
    wd = wd_ref[...].reshape(eps * wd_ref.shape[1], wd_ref.shape[2]).astype(BF16)
    acc_ref[...] += _dot(jnp.concatenate(hids, axis=1), wd)

    @pl.when(s == pl.num_programs(1) - 1)
    def _():
        o_ref[...] = _layer_norm(DEEPNORM_ALPHA * h_ref[...] + acc_ref[...], g_ref[...], b_ref[...])


def _moe(h, gates, wg, wu, wd, sg, su, sd, g, b):
    n, d = h.shape
    n_exp, _, f = wg.shape
    tm = min(1024, n)
    eps = MOE_EXPERTS_PER_STEP
    full = lambda a: pl.BlockSpec(a.shape, lambda i, e: (0,) * a.ndim)
    return pl.pallas_call(
        functools.partial(_moe_body, eps=eps),
        grid=(n // tm, n_exp // eps),
        in_specs=[pl.BlockSpec((tm, d), lambda i, e: (i, 0)), pl.BlockSpec((tm, n_exp), lambda i, e: (i, 0)),
                  pl.BlockSpec((eps, d, f), lambda i, e: (e, 0, 0)), pl.BlockSpec((eps, d, f), lambda i, e: (e, 0, 0)),
                  pl.BlockSpec((eps, f, d), lambda i, e: (e, 0, 0)),
                  full(sg), full(su), full(sd), full(g), full(b)],
        out_specs=pl.BlockSpec((tm, d), lambda i, e: (i, 0)),
        out_shape=jax.ShapeDtypeStruct((n, d), F32),
        scratch_shapes=[pltpu.VMEM((tm, d), F32)],
        compiler_params=_cparams(("parallel", "arbitrary"), LARGE_VMEM_LIMIT),
        name="moe",
    )(h, gates, wg, wu, wd, sg, su, sd, g, b)


def _head_block_diag(q):
    lane_head = lax.broadcasted_iota(jnp.int32, q.shape, 1) // HEAD_DIM
    return jnp.concatenate([jnp.where(lane_head == h, q, 0.0) for h in range(MOBA_HEADS)], axis=0)


def _sample_keys_body(pt_ref, q_ref, *refs, ppstep):
    pages, sc_ref = refs[:ppstep], refs[ppstep]
    qbd = (_head_block_diag(q_ref[0]) * SCALE).astype(BF16)
    for i in range(ppstep):
        kt = pages[i][0].reshape(MOBA_W, PAGE_SIZE)
        sc_ref[0, :, i * PAGE_SIZE:(i + 1) * PAGE_SIZE] = _dot(qbd, kt.astype(BF16))


def _page_spec(n_pages, ppstep, i):
    return pl.BlockSpec((1, MOBA_HEADS, HEAD_DIM, PAGE_SIZE),
                        lambda b, s, pt: (pt[b * n_pages + s * ppstep + i], 0, 0, 0))


def _sample_keys(pt_flat, q3, ckT, n_pages):
    b, t, w = q3.shape
    ppstep = min(SAMPLE_PAGES_PER_STEP, n_pages)
    ht = MOBA_HEADS * t
    grid_spec = pltpu.PrefetchScalarGridSpec(
        num_scalar_prefetch=1,
        grid=(b, n_pages // ppstep),
        in_specs=[pl.BlockSpec((1, t, w), lambda b_, s, pt: (b_, 0, 0))]
        + [_page_spec(n_pages, ppstep, i) for i in range(ppstep)],
        out_specs=pl.BlockSpec((1, ht, ppstep * PAGE_SIZE), lambda b_, s, pt: (b_, 0, s)),
    )
    return pl.pallas_call(
        functools.partial(_sample_keys_body, ppstep=ppstep),
        grid_spec=grid_spec,
        out_shape=jax.ShapeDtypeStruct((b, ht, n_pages * PAGE_SIZE), F32),
        compiler_params=_cparams(("arbitrary", "arbitrary"), LARGE_VMEM_LIMIT),
        name="moba_sample_keys",
    )(pt_flat, q3, *([ckT] * ppstep))


def _sample_values_body(pt_ref, sc_ref, q_ref, kn_ref, vn_ref, *refs, ppstep, n_pages, t_new):
    pages, o_ref, (lg_s, p_s, l_s, acc_s) = refs[:ppstep], refs[ppstep], refs[ppstep + 1:]
    s = pl.program_id(1)
    ppb = MOBA_BLOCK // PAGE_SIZE
    n_fp = n_pages // ppb
    n_sel = min(MOBA_TOPK, n_fp)
    past = n_pages * PAGE_SIZE
    ht = MOBA_HEADS * t_new
    bps = ppstep // ppb

    @pl.when(s == 0)
    def _():
        qbd = _head_block_diag(q_ref[0])
        blane = lax.broadcasted_iota(jnp.int32, (ht, LANES), 1)
        gate = jnp.full((ht, LANES), -jnp.inf, F32)
        for n in range(n_fp):
            gsum = jnp.sum(sc_ref[0, :, n * MOBA_BLOCK:(n + 1) * MOBA_BLOCK], axis=1, keepdims=True)
            gate = jnp.where(blane == n, gsum, gate)
        sel = _topk_mask(gate, n_sel, 1)

        row = lax.broadcasted_iota(jnp.int32, (ht, 1), 0)
        slope = jnp.exp2(-((row // t_new) + 1).astype(F32))
        t_col = row % t_new
        qs = (qbd * SCALE).astype(BF16)
        pad = jnp.zeros((LANES - t_new, MOBA_W), F32)
        knew = jnp.concatenate([kn_ref[0], pad], axis=0).astype(BF16)
        vnew = jnp.concatenate([vn_ref[0], pad], axis=0).astype(BF16)
        l_own = _dot_nt(qs, knew) - slope * (t_col - blane).astype(F32)
        l_own = jnp.where(blane <= t_col, l_own, NEG)
        pos = lax.broadcasted_iota(jnp.int32, (ht, MOBA_BLOCK), 1)
        base = slope * (t_col - pos).astype(F32)
        m_el = jnp.full((ht, LANES), NEG, F32)
        for n in range(n_fp):
            cols = slice(n * MOBA_BLOCK, (n + 1) * MOBA_BLOCK)
            ln = sc_ref[0, :, cols] - base - slope * float(past - n * MOBA_BLOCK)
            ln = jnp.where(sel[:, n:n + 1] > 0.5, ln, NEG)
            m_el = jnp.maximum(m_el, jnp.maximum(ln[:, :LANES], ln[:, LANES:]))
            lg_s[:, cols] = ln
        m = jnp.maximum(jnp.max(l_own, axis=1, keepdims=True), jnp.max(m_el, axis=1, keepdims=True))
        p_own = jnp.exp(l_own - m)
        s_el = jnp.zeros((ht, LANES), F32)
        for n in range(n_fp):
            pn = jnp.exp(lg_s[:, n * MOBA_BLOCK:(n + 1) * MOBA_BLOCK] - m)
            s_el = s_el + (pn[:, :LANES] + pn[:, LANES:])
            p_s[n // bps, :, (n % bps) * MOBA_BLOCK:(n % bps + 1) * MOBA_BLOCK] = pn.astype(BF16)
        lsum = jnp.sum(p_own, axis=1, keepdims=True) + jnp.sum(s_el, axis=1, keepdims=True)
        l_s[...] = jnp.broadcast_to(lsum, (ht, LANES))
        acc_s[...] = _dot(p_own.astype(BF16), vnew)

    vt = jnp.concatenate([pages[i][0].reshape(MOBA_W, PAGE_SIZE).astype(BF16) for i in range(ppstep)], axis=1)
    acc_s[...] += _dot_nt(p_s[s], vt)

    @pl.when(s == pl.num_programs(1) - 1)
    def _():
        o_all = acc_s[...] / l_s[:, 0:1]
        lane_head = lax.broadcasted_iota(jnp.int32, (t_new, MOBA_W), 1) // HEAD_DIM
        out = jnp.zeros((t_new, MOBA_W), F32)
        for h in range(MOBA_HEADS):
            out = out + jnp.where(lane_head == h, o_all[h * t_new:(h + 1) * t_new], 0.0)
        o_ref[0] = out


def _sample_values(pt_flat, scores, q3, kn3, vn3, cvT, n_pages):
    b, t, w = q3.shape
    ppstep = min(SAMPLE_PAGES_PER_STEP, n_pages)
    ht = MOBA_HEADS * t
    n_steps = n_pages // ppstep
    small = lambda a: pl.BlockSpec((1,) + a.shape[1:], lambda b_, s, pt: (b_,) + (0,) * (a.ndim - 1))
    grid_spec = pltpu.PrefetchScalarGridSpec(
        num_scalar_prefetch=1,
        grid=(b, n_steps),
        in_specs=[small(scores), small(q3), small(kn3), small(vn3)]
        + [_page_spec(n_pages, ppstep, i) for i in range(ppstep)],
        out_specs=pl.BlockSpec((1, t, w), lambda b_, s, pt: (b_, 0, 0)),
        scratch_shapes=[pltpu.VMEM((ht, n_pages * PAGE_SIZE), F32),
                        pltpu.VMEM((n_steps, ht, ppstep * PAGE_SIZE), BF16),
                        pltpu.VMEM((ht, LANES), F32),
                        pltpu.VMEM((ht, w), F32)],
    )
    return pl.pallas_call(
        functools.partial(_sample_values_body, ppstep=ppstep, n_pages=n_pages, t_new=t),
        grid_spec=grid_spec,
        out_shape=jax.ShapeDtypeStruct((b, t, w), F32),
        compiler_params=_cparams(("arbitrary", "arbitrary"), LARGE_VMEM_LIMIT),
        name="moba_sample_values",
    )(pt_flat, scores, q3, kn3, vn3, *([cvT] * ppstep))


def _sgu_block_diag(w_s, t_new):
    g = w_s.shape[0]
    reps = SGU_CHUNK // t_new
    small = jnp.tril(w_s)[:, :t_new, :t_new]
    eye = jnp.eye(reps, dtype=w_s.dtype)
    return jnp.einsum("rq,gts->grtqs", eye, small).reshape(g, SGU_CHUNK, SGU_CHUNK)


def _sgu_bias(b_s, rows):
    return jnp.repeat(b_s[:, :rows].T, HEAD_DIM, axis=1)


def _ffn(h1, wrT, rb_col, moe_w, g, b):
    gates = _router(h1, wrT, rb_col).T
    return _moe(h1, gates, *moe_w, g, b)


def kernel(x_prompt, x_sample, mem_prompt, cache_k, cache_v, page_table, cache_mem_k, cache_mem_v, w_in, w_out, w_mem_kv, sgu_ln_g, sgu_ln_b, sgu_w_s, sgu_b_s, ln1_g, ln1_b, w_router, router_bias, w_gate, w_up, w_down, ws_gate, ws_up, ws_down, ln2_g, ln2_b):
    assert w_in.shape[0] == DEPTH and x_prompt.shape[0] == 1
    _, seq, d_model = x_prompt.shape
    db, t_new, _ = x_sample.shape
    n_pages = page_table.shape[1]
    assert n_pages % (MOBA_BLOCK // PAGE_SIZE) == 0 and SGU_CHUNK % t_new == 0
    n_s = db * t_new
    assert n_s % SGU_CHUNK == 0 and seq % (8 * MOBA_BLOCK) == 0 and seq % (ATT_GROUP * MOBA_BLOCK) == 0
    assert n_pages % min(SAMPLE_PAGES_PER_STEP, n_pages) == 0

    w_in_bf = w_in[0].astype(BF16)
    wo_bf = w_out[0].astype(BF16)
    wmem_bf = w_mem_kv[0].astype(BF16)
    sg_g, sg_b = sgu_ln_g, sgu_ln_b
    moe_w = (w_gate[0], w_up[0], w_down[0],
             ws_gate[0].astype(BF16), ws_up[0].astype(BF16), ws_down[0].astype(BF16))
    wrT = jnp.swapaxes(w_router[0], 0, 1)
    rb_col = router_bias[0][:, None]

    xp = x_prompt[0]
    qT, k, k_bf, kTp, vTp, vTg, u, svn, qm = _in_proj_prompt(xp, w_in_bf, sg_g, sg_b)
    mkv = _mem_kv(mem_prompt[0], wmem_bf)
    oT = _moba_prompt(qT, k_bf, vTg, _k_block_mean(k))
    o_c = _mem_attend_prompt(qm, mkv)
    wm_p = jnp.tril(sgu_w_s[0]).astype(BF16)
    h1 = _mix(xp, oT, u, svn, o_c, wm_p, _sgu_bias(sgu_b_s[0], SGU_CHUNK), wo_bf, ln1_g, ln1_b, oa_transposed=True)
    y_prompt = _ffn(h1, wrT, rb_col, moe_w, ln2_g, ln2_b)[None]

    n_pg = seq // PAGE_SIZE
    to_pages = lambda zT: jnp.swapaxes(zT.reshape(1, 1, n_pg, MOBA_HEADS, HEAD_DIM, PAGE_SIZE), -1, -2)
    k_prompt, v_prompt = to_pages(kTp), to_pages(vTp)
    n_mem = mkv.shape[0]
    mem_k_prompt = mkv[:, :MEM_W].reshape(1, 1, n_mem, MEM_HEADS, HEAD_DIM)
    mem_v_prompt = mkv[:, MEM_W:].reshape(1, 1, n_mem, MEM_HEADS, HEAD_DIM)

    xs = x_sample.reshape(n_s, d_model)
    q, k, v, u, svn, qm = _in_proj(xs, w_in_bf, sg_g, sg_b)
    heads = lambda z: z.reshape(db, t_new, MOBA_HEADS, HEAD_DIM).transpose(0, 2, 1, 3)
    k4, v4 = heads(k), heads(v)
    ckT = jnp.swapaxes(cache_k[0], -1, -2)
    cvT = jnp.swapaxes(cache_v[0], -1, -2)
    pt_flat = page_table.reshape(-1)
    three = lambda z: z.reshape(db, t_new, MOBA_W)
    scores = _sample_keys(pt_flat, three(q), ckT, n_pages)
    o_a = _sample_values(pt_flat, scores, three(q), three(k), three(v), cvT, n_pages).reshape(n_s, MOBA_W)
    mkT = cache_mem_k[0].transpose(0, 2, 3, 1)
    mvT = cache_mem_v[0].transpose(0, 2, 3, 1)
    o_c = _mem_attend_sample(qm.reshape(db, t_new, MEM_W), mkT, mvT).reshape(n_s, MEM_W)
    wm_s = _sgu_block_diag(sgu_w_s[0], t_new).astype(BF16)
    bz_s = jnp.tile(_sgu_bias(sgu_b_s[0], t_new), (SGU_CHUNK // t_new, 1))
    h1 = _mix(xs, o_a, u, svn, o_c, wm_s, bz_s, wo_bf, ln1_g, ln1_b)
    y_sample = _ffn(h1, wrT, rb_col, moe_w, ln2_g, ln2_b).reshape(db, t_new, d_model)

    k_sample, v_sample = k4[None], v4[None]
    sgu_v_sample = svn.reshape(1, db, t_new, SGU_W)
    return (y_prompt, y_sample, k_prompt, v_prompt, k_sample, v_sample, mem_k_prompt, mem_v_prompt, sgu_v_sample)
```

```python
import functools

import jax
import jax.numpy as jnp
from jax import lax
from jax.experimental import pallas as pl
from jax.experimental.pallas import tpu as pltpu

F32 = jnp.float32
BF16 = jnp.bfloat16

HEAD_DIM = 64
MOBA_HEADS = 8
SGU_GROUPS = 4
MEM_HEADS = 4
MOBA_W = MOBA_HEADS * HEAD_DIM
SGU_W = SGU_GROUPS * HEAD_DIM
MEM_W = MEM_HEADS * HEAD_DIM
MOBA_BLOCK = 256
MOBA_TOPK = 3
Q_BLOCK = 128
SGU_CHUNK = 128
PAGE_SIZE = 128
N_EXPERT_GROUPS = 8
TOPK_GROUPS = 4
TOP_K = 8
ROUTE_SCALE = 2.5
LN_EPS = 1e-5
NEG = -1e30
DEPTH = 1
DEEPNORM_ALPHA = (2 * DEPTH) ** 0.25
SCALE = HEAD_DIM ** -0.5
LOG2E = 1.4426950408889634
ATT_GROUP = 8
SAMPLE_PAGES_PER_STEP = 64
MOE_EXPERTS_PER_STEP = 4
LANES = 128
VMEM_LIMIT = 48 * 1024 * 1024
LARGE_VMEM_LIMIT = 58 * 1024 * 1024
HIGHEST = lax.Precision.HIGHEST


def _cparams(sem, limit=VMEM_LIMIT):
    return pltpu.CompilerParams(dimension_semantics=sem, vmem_limit_bytes=limit)


def _gelu(x):
    return x * (0.5 * (1.0 + jnp.tanh(0.7978845608028654 * (x + 0.044715 * (x * x * x)))))


def _sigmoid(x):
    return 1.0 / (1.0 + jnp.exp(-x))


def _silu(x):
    return x * _sigmoid(x)


def _layer_norm(x, g, b):
    mu = jnp.mean(x, axis=-1, keepdims=True)
    xc = x - mu
    var = jnp.mean(xc * xc, axis=-1, keepdims=True)
    return xc * lax.rsqrt(var + LN_EPS) * g + b


def _topk_mask(x, k, axis):
    n = x.shape[axis]
    idx = lax.broadcasted_iota(jnp.int32, x.shape, axis).astype(F32)
    sel = jnp.zeros(x.shape, F32)
    for _ in range(k):
        mx = jnp.max(x, axis=axis, keepdims=True)
        first = jnp.min(jnp.where(x == mx, idx, float(n)), axis=axis, keepdims=True)
        hit = idx == first
        sel = jnp.where(hit, 1.0, sel)
        x = jnp.where(hit, -jnp.inf, x)
    return sel


def _dot(a, b):
    return jnp.dot(a, b, preferred_element_type=F32)


def _dot_nt(a, b, precision=None):
    return lax.dot_general(a, b, (((1,), (1,)), ((), ())), precision=precision, preferred_element_type=F32)


def _project(x_ref, w_ref, g_ref, b_ref):
    x = x_ref[...].astype(BF16)
    c0, c1, c2, c3, c4 = MOBA_W, 2 * MOBA_W, 3 * MOBA_W, 3 * MOBA_W + SGU_W, 3 * MOBA_W + 2 * SGU_W
    q = _dot(x, w_ref[:, 0:c0])
    k = _dot(x, w_ref[:, c0:c1])
    v = _dot(x, w_ref[:, c1:c2])
    u = _gelu(_dot(x, w_ref[:, c2:c3]))
    sv = _layer_norm(_gelu(_dot(x, w_ref[:, c3:c4])), g_ref[...], b_ref[...])
    qm = _dot(x, w_ref[:, c4:c4 + MEM_W])
    return q, k, v, u, sv, qm


def _inproj_body(x_ref, w_ref, g_ref, b_ref, *out_refs):
    for ref, val in zip(out_refs, _project(x_ref, w_ref, g_ref, b_ref), strict=True):
        ref[...] = val


def _inproj_prompt_body(x_ref, w_ref, g_ref, b_ref, qT_ref, k_ref, kbf_ref, kTp_ref, vTp_ref, vTg_ref,
                        u_ref, sv_ref, qm_ref):
    q, k, v, u, sv, qm = _project(x_ref, w_ref, g_ref, b_ref)
    tm = q.shape[0]
    qT_ref[...] = q.T
    k_ref[...] = k
    kbf_ref[...] = k.astype(BF16)
    kT, vT = k.T, v.T
    for p in range(tm // PAGE_SIZE):
        kTp_ref[p] = kT[:, p * PAGE_SIZE:(p + 1) * PAGE_SIZE]
        vTp_ref[p] = vT[:, p * PAGE_SIZE:(p + 1) * PAGE_SIZE]
    vT_bf = vT.astype(BF16)
    ones = jnp.ones((ATT_ONES, tm), BF16)
    for c in range(N_PAIRS):
        vTg_ref[0, c * ATT_SLAB:c * ATT_SLAB + ATT_PAIR_W, :] = vT_bf[c * ATT_PAIR_W:(c + 1) * ATT_PAIR_W]
        vTg_ref[0, c * ATT_SLAB + ATT_PAIR_W:(c + 1) * ATT_SLAB, :] = ones
    u_ref[...] = u
    sv_ref[...] = sv
    qm_ref[...] = qm


def _in_proj_prompt(x, w_in_bf, sgu_g, sgu_b):
    n, d = x.shape
    tm = 512
    gk = ATT_GROUP * MOBA_BLOCK
    per_grp = gk // tm
    ppt = tm // PAGE_SIZE
    row = lambda w: pl.BlockSpec((tm, w), lambda i: (i, 0))
    full = lambda a: pl.BlockSpec(a.shape, lambda i: (0,) * a.ndim)
    pages = pl.BlockSpec((ppt, MOBA_W, PAGE_SIZE), lambda i: (i, 0, 0))
    sds = jax.ShapeDtypeStruct
    return pl.pallas_call(
        _inproj_prompt_body,
        grid=(n // tm,),
        in_specs=[row(d), full(w_in_bf), full(sgu_g), full(sgu_b)],
        out_specs=[pl.BlockSpec((MOBA_W, tm), lambda i: (0, i)), row(MOBA_W), row(MOBA_W), pages, pages,
                   pl.BlockSpec((1, N_PAIRS * ATT_SLAB, tm), lambda i: (i // per_grp, 0, i % per_grp)),
                   row(SGU_W), row(SGU_W), row(MEM_W)],
        out_shape=[sds((MOBA_W, n), F32), sds((n, MOBA_W), F32), sds((n, MOBA_W), BF16),
                   sds((n // PAGE_SIZE, MOBA_W, PAGE_SIZE), F32), sds((n // PAGE_SIZE, MOBA_W, PAGE_SIZE), F32),
                   sds((n // gk, N_PAIRS * ATT_SLAB, gk), BF16),
                   sds((n, SGU_W), F32), sds((n, SGU_W), F32), sds((n, MEM_W), F32)],
        compiler_params=_cparams(("parallel",)),
        name="in_proj_prompt",
    )(x, w_in_bf, sgu_g, sgu_b)


def _in_proj(x, w_in_bf, sgu_g, sgu_b):
    n, d = x.shape
    tm = min(512, n)
    in_w = w_in_bf.shape[1]
    row = lambda w: pl.BlockSpec((tm, w), lambda i: (i, 0))
    full = lambda a: pl.BlockSpec(a.shape, lambda i: (0,) * a.ndim)
    widths = (MOBA_W, MOBA_W, MOBA_W, SGU_W, SGU_W, MEM_W)
    return pl.pallas_call(
        _inproj_body,
        grid=(n // tm,),
        in_specs=[row(d), full(w_in_bf), full(sgu_g), full(sgu_b)],
        out_specs=[row(w) for w in widths],
        out_shape=[jax.ShapeDtypeStruct((n, w), F32) for w in widths],
        compiler_params=_cparams(("parallel",)),
        name="in_proj",
    )(x, w_in_bf, sgu_g, sgu_b)


def _memkv_body(m_ref, w_ref, o_ref):
    o_ref[...] = _dot(m_ref[...].astype(BF16), w_ref[...])


def _mem_kv(mem, w_bf):
    n, _ = mem.shape
    return pl.pallas_call(
        _memkv_body,
        out_shape=jax.ShapeDtypeStruct((n, w_bf.shape[1]), F32),
        compiler_params=pltpu.CompilerParams(vmem_limit_bytes=VMEM_LIMIT),
        name="mem_kv",
    )(mem, w_bf)


def _kmean_body(k_ref, o_ref):
    kb = k_ref[...].reshape(8, MOBA_BLOCK, k_ref.shape[1])
    o_ref[...] = jnp.sum(kb, axis=1) * (1.0 / MOBA_BLOCK)


def _k_block_mean(k):
    s, w = k.shape
    nblk = s // MOBA_BLOCK
    return pl.pallas_call(
        _kmean_body,
        grid=(nblk // 8,),
        in_specs=[pl.BlockSpec((8 * MOBA_BLOCK, w), lambda i: (i, 0))],
        out_specs=pl.BlockSpec((8, w), lambda i: (i, 0)),
        out_shape=jax.ShapeDtypeStruct((nblk, w), F32),
        compiler_params=_cparams(("parallel",)),
        name="k_block_mean",
    )(k)


ATT_PAIR_W = 2 * HEAD_DIM
ATT_ONES = 16
ATT_SLAB = ATT_PAIR_W + ATT_ONES
N_PAIRS = MOBA_HEADS // 2


def _moba_prompt_body(qT_ref, k_ref, vTg_ref, km_ref, oT_ref, bias_s, dmat_s, qs_s, *, nblk):
    qi = pl.program_id(0)
    own = qi // (MOBA_BLOCK // Q_BLOCK)
    half = qi % (MOBA_BLOCK // Q_BLOCK)
    w2 = 2 * Q_BLOCK
    gk = ATT_GROUP * MOBA_BLOCK

    lane = lax.broadcasted_iota(jnp.int32, (1, w2), 1)
    tl = lane % Q_BLOCK
    sl = lax.broadcasted_iota(jnp.int32, (MOBA_BLOCK, w2), 0)
    n_iota = lax.broadcasted_iota(jnp.int32, (nblk, w2), 0)
    blk_off = (n_iota * MOBA_BLOCK - qi * Q_BLOCK).astype(F32)
    causal = half * Q_BLOCK + tl - sl >= 0
    z = jnp.zeros((HEAD_DIM, Q_BLOCK), F32)

    state = []
    for c in range(N_PAIRS):
        head = 2 * c + (lane >= Q_BLOCK).astype(jnp.int32)
        slope = jnp.exp2(-(head + 1).astype(F32)) * LOG2E

        @pl.when(qi == 0)
        def _():
            dmat_s[c] = slope * sl.astype(F32)

        qT = qT_ref[c * ATT_PAIR_W:(c + 1) * ATT_PAIR_W, :]
        qbd = jnp.concatenate(
            [jnp.concatenate([qT[:HEAD_DIM], z], axis=1), jnp.concatenate([z, qT[HEAD_DIM:]], axis=1)], axis=0)
        qs_s[c] = (qbd * (SCALE * LOG2E)).astype(BF16)

        gate = jnp.dot(km_ref[:, c * ATT_PAIR_W:(c + 1) * ATT_PAIR_W], qbd, precision=HIGHEST,
                       preferred_element_type=F32)
        gate = jnp.where(n_iota < own, gate, -jnp.inf)
        sel = jnp.logical_and(_topk_mask(gate, MOBA_TOPK, 0) > 0.5, n_iota < own)
        bias_s[c] = jnp.where(sel, slope * blk_off, NEG)

        k_own = k_ref[pl.ds(pl.multiple_of(own * MOBA_BLOCK, MOBA_BLOCK), MOBA_BLOCK),
                      c * ATT_PAIR_W:(c + 1) * ATT_PAIR_W]
        s0 = _dot(k_own, qs_s[c]) + dmat_s[c] - slope * (half * Q_BLOCK).astype(F32)
        s0 = jnp.where(causal, s0, NEG)
        m0 = jnp.max(s0, axis=0, keepdims=True)
        p0 = jnp.exp2(s0 - m0).astype(BF16)
        col0 = pl.multiple_of((own % ATT_GROUP) * MOBA_BLOCK, MOBA_BLOCK)
        res0 = _dot(vTg_ref[own // ATT_GROUP, c * ATT_SLAB:(c + 1) * ATT_SLAB, pl.ds(col0, MOBA_BLOCK)], p0)
        state.append((m0, res0[ATT_PAIR_W:ATT_PAIR_W + 1], res0[:ATT_PAIR_W]))

    def scores(j, c):
        out = []
        for g in range(ATT_GROUP):
            row0 = pl.multiple_of((j * ATT_GROUP + g) * MOBA_BLOCK, MOBA_BLOCK)
            out.append(_dot(k_ref[pl.ds(row0, MOBA_BLOCK), c * ATT_PAIR_W:(c + 1) * ATT_PAIR_W], qs_s[c]) + dmat_s[c])
        return out

    def body(j, carry):
        out = []
        ahead = [scores(j, 0), scores(j, 1)]
        for c in range(N_PAIRS):
            m, l, acc = carry[c]
            parts = ahead.pop(0)
            if c + 2 < N_PAIRS:
                ahead.append(scores(j, c + 2))
            biases = []
            m_new = m
            for g in range(ATT_GROUP):
                bg = bias_s[c, pl.ds(j * ATT_GROUP + g, 1), :]
                m_new = jnp.maximum(m_new, jnp.max(parts[g], axis=0, keepdims=True) + bg)
                biases.append(bg)
            a = jnp.exp2(m - m_new)
            res = None
            for g in range(ATT_GROUP):
                pg = jnp.exp2(parts[g] - (m_new - biases[g])).astype(BF16)
                vt = vTg_ref[j, c * ATT_SLAB:(c + 1) * ATT_SLAB, g * MOBA_BLOCK:(g + 1) * MOBA_BLOCK]
                res = _dot(vt, pg) if res is None else res + _dot(vt, pg)
            out.append((m_new, a * l + res[ATT_PAIR_W:ATT_PAIR_W + 1], a * acc + res[:ATT_PAIR_W]))
        return tuple(out)

    n_trip = (own + (ATT_GROUP - 1)) // ATT_GROUP
    state = lax.fori_loop(0, n_trip, body, tuple(state))
    for c in range(N_PAIRS):
        _, l, acc = state[c]
        inv = 1.0 / l
        r0 = c * ATT_PAIR_W
        oT_ref[r0:r0 + HEAD_DIM, :] = acc[0:HEAD_DIM, 0:Q_BLOCK] * inv[:, 0:Q_BLOCK]
        oT_ref[r0 + HEAD_DIM:r0 + ATT_PAIR_W, :] = acc[HEAD_DIM:, Q_BLOCK:] * inv[:, Q_BLOCK:]


def _moba_prompt(qT, k_bf, vTg, kmean):
    w, s = qT.shape
    nblk = kmean.shape[0]
    resident = lambda a: pl.BlockSpec(a.shape, lambda i: (0,) * a.ndim, pipeline_mode=pl.Buffered(1))
    return pl.pallas_call(
        functools.partial(_moba_prompt_body, nblk=nblk),
        grid=(s // Q_BLOCK,),
        in_specs=[pl.BlockSpec((w, Q_BLOCK), lambda i: (0, i)), resident(k_bf), resident(vTg), resident(kmean)],
        out_specs=pl.BlockSpec((w, Q_BLOCK), lambda i: (0, i)),
        out_shape=jax.ShapeDtypeStruct((w, s), F32),
        scratch_shapes=[pltpu.VMEM((N_PAIRS, nblk, 2 * Q_BLOCK), F32),
                        pltpu.VMEM((N_PAIRS, MOBA_BLOCK, 2 * Q_BLOCK), F32),
                        pltpu.VMEM((N_PAIRS, ATT_PAIR_W, 2 * Q_BLOCK), BF16)],
        compiler_params=_cparams(("arbitrary",), LARGE_VMEM_LIMIT),
        name="moba_prompt",
    )(qT, k_bf, vTg, kmean)


def _softmax_rows(logits):
    m = jnp.max(logits, axis=-1, keepdims=True)
    e = jnp.exp(logits - m)
    return e / jnp.sum(e, axis=-1, keepdims=True)


def _mem_prompt_body(qm_ref, mkv_ref, o_ref):
    outs = []
    for h in range(MEM_HEADS):
        lo, hi = h * HEAD_DIM, (h + 1) * HEAD_DIM
        q = (qm_ref[:, lo:hi] * SCALE).astype(BF16)
        mk = mkv_ref[:, lo:hi].astype(BF16)
        mv = mkv_ref[:, MEM_W + lo:MEM_W + hi].astype(BF16)
        p = _softmax_rows(_dot_nt(q, mk))
        outs.append(_dot(p.astype(BF16), mv))
    o_ref[...] = jnp.concatenate(outs, axis=1)


def _mem_attend_prompt(qm, mkv):
    n, w = qm.shape
    tm = min(512, n)
    return pl.pallas_call(
        _mem_prompt_body,
        grid=(n // tm,),
        in_specs=[pl.BlockSpec((tm, w), lambda i: (i, 0)), pl.BlockSpec(mkv.shape, lambda i: (0, 0))],
        out_specs=pl.BlockSpec((tm, w), lambda i: (i, 0)),
        out_shape=jax.ShapeDtypeStruct((n, w), F32),
        compiler_params=_cparams(("parallel",)),
        name="mem_attend_prompt",
    )(qm, mkv)


def _mem_sample_body(qm_ref, mkT_ref, mvT_ref, o_ref, *, sb):
    for i in range(sb):
        outs = []
        for h in range(MEM_HEADS):
            lo, hi = h * HEAD_DIM, (h + 1) * HEAD_DIM
            q = (qm_ref[i, :, lo:hi] * SCALE).astype(BF16)
            p = _softmax_rows(_dot(q, mkT_ref[i, h].astype(BF16)))
            outs.append(_dot_nt(p.astype(BF16), mvT_ref[i, h].astype(BF16)))
        o_ref[i] = jnp.concatenate(outs, axis=1)


def _mem_attend_sample(qm3, mkT, mvT):
    b, t, w = qm3.shape
    sb = min(8, b)
    blk = (sb,) + mkT.shape[1:]
    return pl.pallas_call(
        functools.partial(_mem_sample_body, sb=sb),
        grid=(b // sb,),
        in_specs=[pl.BlockSpec((sb, t, w), lambda i: (i, 0, 0)),
                  pl.BlockSpec(blk, lambda i: (i, 0, 0, 0)), pl.BlockSpec(blk, lambda i: (i, 0, 0, 0))],
        out_specs=pl.BlockSpec((sb, t, w), lambda i: (i, 0, 0)),
        out_shape=jax.ShapeDtypeStruct((b, t, w), F32),
        compiler_params=_cparams(("parallel",)),
        name="mem_attend_sample",
    )(qm3, mkT, mvT)


def _mix_body(x_ref, oa_ref, u_ref, vn_ref, oc_ref, wm_ref, bz_ref, wo_ref, g_ref, b_ref, h_ref, *, tm, oa_transposed):
    lane_grp = lax.broadcasted_iota(jnp.int32, (SGU_CHUNK, SGU_W), 1) // HEAD_DIM
    zs = []
    for c in range(tm // SGU_CHUNK):
        vn = vn_ref[c * SGU_CHUNK:(c + 1) * SGU_CHUNK, :].astype(BF16)
        z = bz_ref[...]
        for g in range(SGU_GROUPS):
            z = z + jnp.where(lane_grp == g, _dot(wm_ref[g], vn), 0.0)
        zs.append(z)
    ob = u_ref[...] * jnp.concatenate(zs, axis=0)
    oa = oa_ref[...].T if oa_transposed else oa_ref[...]
    y = _dot(oa.astype(BF16), wo_ref[0:MOBA_W, :])
    y = y + _dot(ob.astype(BF16), wo_ref[MOBA_W:MOBA_W + SGU_W, :])
    y = y + _dot(oc_ref[...].astype(BF16), wo_ref[MOBA_W + SGU_W:, :])
    h_ref[...] = _layer_norm(DEEPNORM_ALPHA * x_ref[...] + y, g_ref[...], b_ref[...])


def _mix(x, oa, u, vn, oc, wmat_bf, bz, wo_bf, g, b, oa_transposed=False):
    n, d = x.shape
    tm = min(512, n)
    row = lambda a: pl.BlockSpec((tm, a.shape[1]), lambda i: (i, 0))
    full = lambda a: pl.BlockSpec(a.shape, lambda i: (0,) * a.ndim)
    oa_spec = pl.BlockSpec((oa.shape[0], tm), lambda i: (0, i)) if oa_transposed else row(oa)
    return pl.pallas_call(
        functools.partial(_mix_body, tm=tm, oa_transposed=oa_transposed),
        grid=(n // tm,),
        in_specs=[row(x), oa_spec, row(u), row(vn), row(oc), full(wmat_bf), full(bz), full(wo_bf), full(g), full(b)],
        out_specs=row(x),
        out_shape=jax.ShapeDtypeStruct((n, d), F32),
        compiler_params=_cparams(("parallel",)),
        name="mix",
    )(x, oa, u, vn, oc, wmat_bf, bz, wo_bf, g, b)


def _rank_rows(x, n_rows, axis):
    idx = lax.broadcasted_iota(jnp.int32, x.shape, axis)
    rank = jnp.zeros(x.shape, F32)
    for j in range(n_rows):
        xj = lax.slice_in_dim(x, j, j + 1, axis=axis)
        tie = jnp.where(idx > j, 1.0, 0.0)
        rank = rank + jnp.where(xj > x, 1.0, jnp.where(xj == x, tie, 0.0))
    return rank


def _router_body(h_ref, wrT_ref, rb_ref, o_ref, *, n_exp):
    tm = h_ref.shape[0]
    gsz = n_exp // N_EXPERT_GROUPS
    scores = _sigmoid(_dot_nt(wrT_ref[...], h_ref[...], precision=HIGHEST))
    biased = scores + rb_ref[...]
    b3 = biased.reshape(N_EXPERT_GROUPS, gsz, tm)
    top2 = jnp.sum(jnp.where(_rank_rows(b3, gsz, 1) < 2.0, b3, 0.0), axis=1, keepdims=True)
    gs = jnp.broadcast_to(top2, b3.shape).reshape(n_exp, tm)
    grp = lax.broadcasted_iota(jnp.int32, (n_exp, tm), 0) // gsz
    grank = jnp.zeros((n_exp, tm), F32)
    for g2 in range(N_EXPERT_GROUPS):
        xg = gs[g2 * gsz:g2 * gsz + 1, :]
        tie = jnp.where(grp > g2, 1.0, 0.0)
        grank = grank + jnp.where(xg > gs, 1.0, jnp.where(xg == gs, tie, 0.0))
    masked = jnp.where(grank < float(TOPK_GROUPS), biased, -jnp.inf)
    w = _topk_mask(masked, TOP_K, 0) * scores
    o_ref[...] = w / jnp.sum(w, axis=0, keepdims=True) * ROUTE_SCALE


def _router(h, wrT, rb_col):
    n, d = h.shape
    n_exp = wrT.shape[0]
    tm = min(512, n)
    return pl.pallas_call(
        functools.partial(_router_body, n_exp=n_exp),
        grid=(n // tm,),
        in_specs=[pl.BlockSpec((tm, d), lambda i: (i, 0)), pl.BlockSpec(wrT.shape, lambda i: (0, 0)),
                  pl.BlockSpec(rb_col.shape, lambda i: (0, 0))],
        out_specs=pl.BlockSpec((n_exp, tm), lambda i: (0, i)),
        out_shape=jax.ShapeDtypeStruct((n_exp, n), F32),
        compiler_params=_cparams(("parallel",)),
        name="router",
    )(h, wrT, rb_col)


def _moe_body(h_ref, gt_ref, wg_ref, wu_ref, wd_ref, sg_ref, su_ref, sd_ref, g_ref, b_ref, o_ref, acc_ref, *, eps):
    s = pl.program_id(1)
    hb = h_ref[...].astype(BF16)

    @pl.when(s == 0)
    def _():
        hs = _silu(_dot(hb, sg_ref[...])) * _dot(hb, su_ref[...])
        acc_ref[...] = _dot(hs.astype(BF16), sd_ref[...])

    lane = lax.broadcasted_iota(jnp.int32, gt_ref.shape, 1)
    hids = []
    for j in range(eps):
        gcol = jnp.sum(jnp.where(lane == s * eps + j, gt_ref[...], 0.0), axis=1, keepdims=True)
        hid = _silu(_dot(hb, wg_ref[j].astype(BF16))) * _dot(hb, wu_ref[j].astype(BF16)) * gcol
        hids.append(hid.astype(BF16))
    wd = wd_ref[...].reshape(eps * wd_ref.shape[1], wd_ref.shape[2]).astype(BF16)
    acc_ref[...] += _dot(jnp.concatenate(hids, axis=1), wd)

    @pl.when(s == pl.num_programs(1) - 1)
    def _():
        o_ref[...] = _layer_norm(DEEPNORM_ALPHA * h_ref[...] + acc_ref[...], g_ref[...], b_ref[...])


def _moe(h, gates, wg, wu, wd, sg, su, sd, g, b):
    n, d = h.shape
    n_exp, _, f = wg.shape
    tm = min(1024, n)
    eps = MOE_EXPERTS_PER_STEP
    full = lambda a: pl.BlockSpec(a.shape, lambda i, e: (0,) * a.ndim)
    return pl.pallas_call(
        functools.partial(_moe_body, eps=eps),
        grid=(n // tm, n_exp // eps),
        in_specs=[pl.BlockSpec((tm, d), lambda i, e: (i, 0)), pl.BlockSpec((tm, n_exp), lambda i, e: (i, 0)),
                  pl.BlockSpec((eps, d, f), lambda i, e: (e, 0, 0)), pl.BlockSpec((eps, d, f), lambda i, e: (e, 0, 0)),
                  pl.BlockSpec((eps, f, d), lambda i, e: (e, 0, 0)),
                  full(sg), full(su), full(sd), full(g), full(b)],
        out_specs=pl.BlockSpec((tm, d), lambda i, e: (i, 0)),
        out_shape=jax.ShapeDtypeStruct((n, d), F32),
        scratch_shapes=[pltpu.VMEM((tm, d), F32)],
        compiler_params=_cparams(("parallel", "arbitrary"), LARGE_VMEM_LIMIT),
        name="moe",
    )(h, gates, wg, wu, wd, sg, su, sd, g, b)


def _head_block_diag(q):
    lane_head = lax.broadcasted_iota(jnp.int32, q.shape, 1) // HEAD_DIM
    return jnp.concatenate([jnp.where(lane_head == h, q, 0.0) for h in range(MOBA_HEADS)], axis=0)


def _sample_keys_body(pt_ref, q_ref, *refs, ppstep):
    pages, sc_ref = refs[:ppstep], refs[ppstep]
    qbd = (_head_block_diag(q_ref[0]) * SCALE).astype(BF16)
    for i in range(ppstep):
        kt = pages[i][0].reshape(MOBA_W, PAGE_SIZE)
        sc_ref[0, :, i * PAGE_SIZE:(i + 1) * PAGE_SIZE] = _dot(qbd, kt.astype(BF16))


def _page_spec(n_pages, ppstep, i):
    return pl.BlockSpec((1, MOBA_HEADS, HEAD_DIM, PAGE_SIZE),
                        lambda b, s, pt: (pt[b * n_pages + s * ppstep + i], 0, 0, 0))


def _sample_keys(pt_flat, q3, ckT, n_pages):
    b, t, w = q3.shape
    ppstep = min(SAMPLE_PAGES_PER_STEP, n_pages)
    ht = MOBA_HEADS * t
    grid_spec = pltpu.PrefetchScalarGridSpec(
        num_scalar_prefetch=1,
        grid=(b, n_pages // ppstep),
        in_specs=[pl.BlockSpec((1, t, w), lambda b_, s, pt: (b_, 0, 0))]
        + [_page_spec(n_pages, ppstep, i) for i in range(ppstep)],
        out_specs=pl.BlockSpec((1, ht, ppstep * PAGE_SIZE), lambda b_, s, pt: (b_, 0, s)),
    )
    return pl.pallas_call(
        functools.partial(_sample_keys_body, ppstep=ppstep),
        grid_spec=grid_spec,
        out_shape=jax.ShapeDtypeStruct((b, ht, n_pages * PAGE_SIZE), F32),
        compiler_params=_cparams(("arbitrary", "arbitrary"), LARGE_VMEM_LIMIT),
        name="moba_sample_keys",
    )(pt_flat, q3, *([ckT] * ppstep))


def _sample_values_body(pt_ref, sc_ref, q_ref, kn_ref, vn_ref, *refs, ppstep, n_pages, t_new):
    pages, o_ref, (lg_s, p_s, l_s, acc_s) = refs[:ppstep], refs[ppstep], refs[ppstep + 1:]
    s = pl.program_id(1)
    ppb = MOBA_BLOCK // PAGE_SIZE
    n_fp = n_pages // ppb
    n_sel = min(MOBA_TOPK, n_fp)
    past = n_pages * PAGE_SIZE
    ht = MOBA_HEADS * t_new
    bps = ppstep // ppb

    @pl.when(s == 0)
    def _():
        qbd = _head_block_diag(q_ref[0])
        blane = lax.broadcasted_iota(jnp.int32, (ht, LANES), 1)
        gate = jnp.full((ht, LANES), -jnp.inf, F32)
        for n in range(n_fp):
            gsum = jnp.sum(sc_ref[0, :, n * MOBA_BLOCK:(n + 1) * MOBA_BLOCK], axis=1, keepdims=True)
            gate = jnp.where(blane == n, gsum, gate)
        sel = _topk_mask(gate, n_sel, 1)

        row = lax.broadcasted_iota(jnp.int32, (ht, 1), 0)
        slope = jnp.exp2(-((row // t_new) + 1).astype(F32))
        t_col = row % t_new
        qs = (qbd * SCALE).astype(BF16)
        pad = jnp.zeros((LANES - t_new, MOBA_W), F32)
        knew = jnp.concatenate([kn_ref[0], pad], axis=0).astype(BF16)
        vnew = jnp.concatenate([vn_ref[0], pad], axis=0).astype(BF16)
        l_own = _dot_nt(qs, knew) - slope * (t_col - blane).astype(F32)
        l_own = jnp.where(blane <= t_col, l_own, NEG)
        pos = lax.broadcasted_iota(jnp.int32, (ht, MOBA_BLOCK), 1)
        base = slope * (t_col - pos).astype(F32)
        m_el = jnp.full((ht, LANES), NEG, F32)
        for n in range(n_fp):
            cols = slice(n * MOBA_BLOCK, (n + 1) * MOBA_BLOCK)
            ln = sc_ref[0, :, cols] - base - slope * float(past - n * MOBA_BLOCK)
            ln = jnp.where(sel[:, n:n + 1] > 0.5, ln, NEG)
            m_el = jnp.maximum(m_el, jnp.maximum(ln[:, :LANES], ln[:, LANES:]))
            lg_s[:, cols] = ln
        m = jnp.maximum(jnp.max(l_own, axis=1, keepdims=True), jnp.max(m_el, axis=1, keepdims=True))
        p_own = jnp.exp(l_own - m)
        s_el = jnp.zeros((ht, LANES), F32)
        for n in range(n_fp):
            pn = jnp.exp(lg_s[:, n * MOBA_BLOCK:(n + 1) * MOBA_BLOCK] - m)
            s_el = s_el + (pn[:, :LANES] + pn[:, LANES:])
            p_s[n // bps, :, (n % bps) * MOBA_BLOCK:(n % bps + 1) * MOBA_BLOCK] = pn.astype(BF16)
        lsum = jnp.sum(p_own, axis=1, keepdims=True) + jnp.sum(s_el, axis=1, keepdims=True)
        l_s[...] = jnp.broadcast_to(lsum, (ht, LANES))
        acc_s[...] = _dot(p_own.astype(BF16), vnew)

    vt = jnp.concatenate([pages[i][0].reshape(MOBA_W, PAGE_SIZE).astype(BF16) for i in range(ppstep)], axis=1)
    acc_s[...] += _dot_nt(p_s[s], vt)

    @pl.when(s == pl.num_programs(1) - 1)
    def _():
        o_all = acc_s[...] / l_s[:, 0:1]
        lane_head = lax.broadcasted_iota(jnp.int32, (t_new, MOBA_W), 1) // HEAD_DIM
        out = jnp.zeros((t_new, MOBA_W), F32)
        for h in range(MOBA_HEADS):
            out = out + jnp.where(lane_head == h, o_all[h * t_new:(h + 1) * t_new], 0.0)
        o_ref[0] = out


def _sample_values(pt_flat, scores, q3, kn3, vn3, cvT, n_pages):
    b, t, w = q3.shape
    ppstep = min(SAMPLE_PAGES_PER_STEP, n_pages)
    ht = MOBA_HEADS * t
    n_steps = n_pages // ppstep
    small = lambda a: pl.BlockSpec((1,) + a.shape[1:], lambda b_, s, pt: (b_,) + (0,) * (a.ndim - 1))
    grid_spec = pltpu.PrefetchScalarGridSpec(
        num_scalar_prefetch=1,
        grid=(b, n_steps),
        in_specs=[small(scores), small(q3), small(kn3), small(vn3)]
        + [_page_spec(n_pages, ppstep, i) for i in range(ppstep)],
        out_specs=pl.BlockSpec((1, t, w), lambda b_, s, pt: (b_, 0, 0)),
        scratch_shapes=[pltpu.VMEM((ht, n_pages * PAGE_SIZE), F32),
                        pltpu.VMEM((n_steps, ht, ppstep * PAGE_SIZE), BF16),
                        pltpu.VMEM((ht, LANES), F32),
                        pltpu.VMEM((ht, w), F32)],
    )
    return pl.pallas_call(
        functools.partial(_sample_values_body, ppstep=ppstep, n_pages=n_pages, t_new=t),
        grid_spec=grid_spec,
        out_shape=jax.ShapeDtypeStruct((b, t, w), F32),
        compiler_params=_cparams(("arbitrary", "arbitrary"), LARGE_VMEM_LIMIT),
        name="moba_sample_values",
    )(pt_flat, scores, q3, kn3, vn3, *([cvT] * ppstep))


def _sgu_block_diag(w_s, t_new):
    g = w_s.shape[0]
    reps = SGU_CHUNK // t_new
    small = jnp.tril(w_s)[:, :t_new, :t_new]
    eye = jnp.eye(reps, dtype=w_s.dtype)
    return jnp.einsum("rq,gts->grtqs", eye, small).reshape(g, SGU_CHUNK, SGU_CHUNK)


def _sgu_bias(b_s, rows):
    return jnp.repeat(b_s[:, :rows].T, HEAD_DIM, axis=1)


def _ffn(h1, wrT, rb_col, moe_w, g, b):
    gates = _router(h1, wrT, rb_col).T
    return _moe(h1, gates, *moe_w, g, b)


def kernel(x_prompt, x_sample, mem_prompt, cache_k, cache_v, page_table, cache_mem_k, cache_mem_v, w_in, w_out, w_mem_kv, sgu_ln_g, sgu_ln_b, sgu_w_s, sgu_b_s, ln1_g, ln1_b, w_router, router_bias, w_gate, w_up, w_down, ws_gate, ws_up, ws_down, ln2_g, ln2_b):
    assert w_in.shape[0] == DEPTH and x_prompt.shape[0] == 1
    _, seq, d_model = x_prompt.shape
    db, t_new, _ = x_sample.shape
    n_pages = page_table.shape[1]
    assert n_pages % (MOBA_BLOCK // PAGE_SIZE) == 0 and SGU_CHUNK % t_new == 0
    n_s = db * t_new
    assert n_s % SGU_CHUNK == 0 and seq % (8 * MOBA_BLOCK) == 0 and seq % (ATT_GROUP * MOBA_BLOCK) == 0
    assert n_pages % min(SAMPLE_PAGES_PER_STEP, n_pages) == 0

    w_in_bf = w_in[0].astype(BF16)
    wo_bf = w_out[0].astype(BF16)
    wmem_bf = w_mem_kv[0].astype(BF16)
    sg_g, sg_b = sgu_ln_g, sgu_ln_b
    moe_w = (w_gate[0], w_up[0], w_down[0],
             ws_gate[0].astype(BF16), ws_up[0].astype(BF16), ws_down[0].astype(BF16))
    wrT = jnp.swapaxes(w_router[0], 0, 1)
    rb_col = router_bias[0][:, None]

    xp = x_prompt[0]
    qT, k, k_bf, kTp, vTp, vTg, u, svn, qm = _in_proj_prompt(xp, w_in_bf, sg_g, sg_b)
    mkv = _mem_kv(mem_prompt[0], wmem_bf)
    oT = _moba_prompt(qT, k_bf, vTg, _k_block_mean(k))
    o_c = _mem_attend_prompt(qm, mkv)
    wm_p = jnp.tril(sgu_w_s[0]).astype(BF16)
    h1 = _mix(xp, oT, u, svn, o_c, wm_p, _sgu_bias(sgu_b_s[0], SGU_CHUNK), wo_bf, ln1_g, ln1_b, oa_transposed=True)
    y_prompt = _ffn(h1, wrT, rb_col, moe_w, ln2_g, ln2_b)[None]

    n_pg = seq // PAGE_SIZE
    to_pages = lambda zT: jnp.swapaxes(zT.reshape(1, 1, n_pg, MOBA_HEADS, HEAD_DIM, PAGE_SIZE), -1, -2)
    k_prompt, v_prompt = to_pages(kTp), to_pages(vTp)
    n_mem = mkv.shape[0]
    mem_k_prompt = mkv[:, :MEM_W].reshape(1, 1, n_mem, MEM_HEADS, HEAD_DIM)
    mem_v_prompt = mkv[:, MEM_W:].reshape(1, 1, n_mem, MEM_HEADS, HEAD_DIM)

    xs = x_sample.reshape(n_s, d_model)
    q, k, v, u, svn, qm = _in_proj(xs, w_in_bf, sg_g, sg_b)
    heads = lambda z: z.reshape(db, t_new, MOBA_HEADS, HEAD_DIM).transpose(0, 2, 1, 3)
    k4, v4 = heads(k), heads(v)
    ckT = jnp.swapaxes(cache_k[0], -1, -2)
    cvT = jnp.swapaxes(cache_v[0], -1, -2)
    pt_flat = page_table.reshape(-1)
    three = lambda z: z.reshape(db, t_new, MOBA_W)
    scores = _sample_keys(pt_flat, three(q), ckT, n_pages)
    o_a = _sample_values(pt_flat, scores, three(q), three(k), three(v), cvT, n_pages).reshape(n_s, MOBA_W)
    mkT = cache_mem_k[0].transpose(0, 2, 3, 1)
    mvT = cache_mem_v[0].transpose(0, 2, 3, 1)
    o_c = _mem_attend_sample(qm.reshape(db, t_new, MEM_W), mkT, mvT).reshape(n_s, MEM_W)
    wm_s = _sgu_block_diag(sgu_w_s[0], t_new).astype(BF16)
    bz_s = jnp.tile(_sgu_bias(sgu_b_s[0], t_new), (SGU_CHUNK // t_new, 1))
    h1 = _mix(xs, o_a, u, svn, o_c, wm_s, bz_s, wo_bf, ln1_g, ln1_b)
    y_sample = _ffn(h1, wrT, rb_col, moe_w, ln2_g, ln2_b).reshape(db, t_new, d_model)

    k_sample, v_sample = k4[None], v4[None]
    sgu_v_sample = svn.reshape(1, db, t_new, SGU_W)
    return (y_prompt, y_sample, k_prompt, v_prompt, k_sample, v_sample, mem_k_prompt, mem_v_prompt, sgu_v_sample)
```

```python
import functools

import jax
import jax.numpy as jnp
from jax import lax
from jax.experimental import pallas as pl
from jax.experimental.pallas import tpu as pltpu

F32 = jnp.float32
BF16 = jnp.bfloat16

HEAD_DIM = 64
MOBA_HEADS = 8
SGU_GROUPS = 4
MEM_HEADS = 4
MOBA_W = MOBA_HEADS * HEAD_DIM
SGU_W = SGU_GROUPS * HEAD_DIM
MEM_W = MEM_HEADS * HEAD_DIM
MOBA_BLOCK = 256
MOBA_TOPK = 3
Q_BLOCK = 128
SGU_CHUNK = 128
PAGE_SIZE = 128
N_EXPERT_GROUPS = 8
TOPK_GROUPS = 4
TOP_K = 8
ROUTE_SCALE = 2.5
LN_EPS = 1e-5
NEG = -1e30
DEPTH = 1
DEEPNORM_ALPHA = (2 * DEPTH) ** 0.25
SCALE = HEAD_DIM ** -0.5
LOG2E = 1.4426950408889634
ATT_GROUP = 8
SAMPLE_PAGES_PER_STEP = 64
MOE_EXPERTS_PER_STEP = 4
LANES = 128
VMEM_LIMIT = 48 * 1024 * 1024
LARGE_VMEM_LIMIT = 58 * 1024 * 1024
HIGHEST = lax.Precision.HIGHEST


def _cparams(sem, limit=VMEM_LIMIT):
    return pltpu.CompilerParams(dimension_semantics=sem, vmem_limit_bytes=limit)


def _gelu(x):
    return x * (0.5 * (1.0 + jnp.tanh(0.7978845608028654 * (x + 0.044715 * (x * x * x)))))


def _sigmoid(x):
    return 1.0 / (1.0 + jnp.exp(-x))


def _silu(x):
    return x * _sigmoid(x)


def _layer_norm(x, g, b):
    mu = jnp.mean(x, axis=-1, keepdims=True)
    xc = x - mu
    var = jnp.mean(xc * xc, axis=-1, keepdims=True)
    return xc * lax.rsqrt(var + LN_EPS) * g + b


def _topk_mask(x, k, axis):
    n = x.shape[axis]
    idx = lax.broadcasted_iota(jnp.int32, x.shape, axis).astype(F32)
    sel = jnp.zeros(x.shape, F32)
    for _ in range(k):
        mx = jnp.max(x, axis=axis, keepdims=True)
        first = jnp.min(jnp.where(x == mx, idx, float(n)), axis=axis, keepdims=True)
        hit = idx == first
        sel = jnp.where(hit, 1.0, sel)
        x = jnp.where(hit, -jnp.inf, x)
    return sel


def _dot(a, b):
    return jnp.dot(a, b, preferred_element_type=F32)


def _dot_nt(a, b, precision=None):
    return lax.dot_general(a, b, (((1,), (1,)), ((), ())), precision=precision, preferred_element_type=F32)


def _project(x_ref, w_ref, g_ref, b_ref):
    x = x_ref[...].astype(BF16)
    c0, c1, c2, c3, c4 = MOBA_W, 2 * MOBA_W, 3 * MOBA_W, 3 * MOBA_W + SGU_W, 3 * MOBA_W + 2 * SGU_W
    q = _dot(x, w_ref[:, 0:c0])
    k = _dot(x, w_ref[:, c0:c1])
    v = _dot(x, w_ref[:, c1:c2])
    u = _gelu(_dot(x, w_ref[:, c2:c3]))
    sv = _layer_norm(_gelu(_dot(x, w_ref[:, c3:c4])), g_ref[...], b_ref[...])
    qm = _dot(x, w_ref[:, c4:c4 + MEM_W])
    return q, k, v, u, sv, qm


def _inproj_body(x_ref, w_ref, g_ref, b_ref, *out_refs):
    for ref, val in zip(out_refs, _project(x_ref, w_ref, g_ref, b_ref), strict=True):
        ref[...] = val


def _inproj_prompt_body(x_ref, w_ref, g_ref, b_ref, qT_ref, k_ref, kbf_ref, kTp_ref, vTp_ref, vTg_ref,
                        u_ref, sv_ref, qm_ref):
    q, k, v, u, sv, qm = _project(x_ref, w_ref, g_ref, b_ref)
    tm = q.shape[0]
    qT_ref[...] = q.T
    k_ref[...] = k
    kbf_ref[...] = k.astype(BF16)
    kT, vT = k.T, v.T
    for p in range(tm // PAGE_SIZE):
        kTp_ref[p] = kT[:, p * PAGE_SIZE:(p + 1) * PAGE_SIZE]
        vTp_ref[p] = vT[:, p * PAGE_SIZE:(p + 1) * PAGE_SIZE]
    vT_bf = vT.astype(BF16)
    ones = jnp.ones((ATT_ONES, tm), BF16)
    for c in range(N_PAIRS):
        vTg_ref[0, c * ATT_SLAB:c * ATT_SLAB + ATT_PAIR_W, :] = vT_bf[c * ATT_PAIR_W:(c + 1) * ATT_PAIR_W]
        vTg_ref[0, c * ATT_SLAB + ATT_PAIR_W:(c + 1) * ATT_SLAB, :] = ones
    u_ref[...] = u
    sv_ref[...] = sv
    qm_ref[...] = qm


def _in_proj_prompt(x, w_in_bf, sgu_g, sgu_b):
    n, d = x.shape
    tm = 512
    gk = ATT_GROUP * MOBA_BLOCK
    per_grp = gk // tm
    ppt = tm // PAGE_SIZE
    row = lambda w: pl.BlockSpec((tm, w), lambda i: (i, 0))
    full = lambda a: pl.BlockSpec(a.shape, lambda i: (0,) * a.ndim)
    pages = pl.BlockSpec((ppt, MOBA_W, PAGE_SIZE), lambda i: (i, 0, 0))
    sds = jax.ShapeDtypeStruct
    return pl.pallas_call(
        _inproj_prompt_body,
        grid=(n // tm,),
        in_specs=[row(d), full(w_in_bf), full(sgu_g), full(sgu_b)],
        out_specs=[pl.BlockSpec((MOBA_W, tm), lambda i: (0, i)), row(MOBA_W), row(MOBA_W), pages, pages,
                   pl.BlockSpec((1, N_PAIRS * ATT_SLAB, tm), lambda i: (i // per_grp, 0, i % per_grp)),
                   row(SGU_W), row(SGU_W), row(MEM_W)],
        out_shape=[sds((MOBA_W, n), F32), sds((n, MOBA_W), F32), sds((n, MOBA_W), BF16),
                   sds((n // PAGE_SIZE, MOBA_W, PAGE_SIZE), F32), sds((n // PAGE_SIZE, MOBA_W, PAGE_SIZE), F32),
                   sds((n // gk, N_PAIRS * ATT_SLAB, gk), BF16),
                   sds((n, SGU_W), F32), sds((n, SGU_W), F32), sds((n, MEM_W), F32)],
        compiler_params=_cparams(("parallel",)),
        name="in_proj_prompt",
    )(x, w_in_bf, sgu_g, sgu_b)


def _in_proj(x, w_in_bf, sgu_g, sgu_b):
    n, d = x.shape
    tm = min(512, n)
    in_w = w_in_bf.shape[1]
    row = lambda w: pl.BlockSpec((tm, w), lambda i: (i, 0))
    full = lambda a: pl.BlockSpec(a.shape, lambda i: (0,) * a.ndim)
    widths = (MOBA_W, MOBA_W, MOBA_W, SGU_W, SGU_W, MEM_W)
    return pl.pallas_call(
        _inproj_body,
        grid=(n // tm,),
        in_specs=[row(d), full(w_in_bf), full(sgu_g), full(sgu_b)],
        out_specs=[row(w) for w in widths],
        out_shape=[jax.ShapeDtypeStruct((n, w), F32) for w in widths],
        compiler_params=_cparams(("parallel",)),
        name="in_proj",
    )(x, w_in_bf, sgu_g, sgu_b)


def _memkv_body(m_ref, w_ref, o_ref):
    o_ref[...] = _dot(m_ref[...].astype(BF16), w_ref[...])


def _mem_kv(mem, w_bf):
    n, _ = mem.shape
    return pl.pallas_call(
        _memkv_body,
        out_shape=jax.ShapeDtypeStruct((n, w_bf.shape[1]), F32),
        compiler_params=pltpu.CompilerParams(vmem_limit_bytes=VMEM_LIMIT),
        name="mem_kv",
    )(mem, w_bf)


def _kmean_body(k_ref, o_ref):
    kb = k_ref[...].reshape(8, MOBA_BLOCK, k_ref.shape[1])
    o_ref[...] = jnp.sum(kb, axis=1) * (1.0 / MOBA_BLOCK)


def _k_block_mean(k):
    s, w = k.shape
    nblk = s // MOBA_BLOCK
    return pl.pallas_call(
        _kmean_body,
        grid=(nblk // 8,),
        in_specs=[pl.BlockSpec((8 * MOBA_BLOCK, w), lambda i: (i, 0))],
        out_specs=pl.BlockSpec((8, w), lambda i: (i, 0)),
        out_shape=jax.ShapeDtypeStruct((nblk, w), F32),
        compiler_params=_cparams(("parallel",)),
        name="k_block_mean",
    )(k)


ATT_PAIR_W = 2 * HEAD_DIM
ATT_ONES = 16
ATT_SLAB = ATT_PAIR_W + ATT_ONES
N_PAIRS = MOBA_HEADS // 2


def _moba_prompt_body(qT_ref, k_ref, vTg_ref, km_ref, oT_ref, bias_s, dmat_s, qs_s, *, nblk):
    qi = pl.program_id(0)
    own = qi // (MOBA_BLOCK // Q_BLOCK)
    half = qi % (MOBA_BLOCK // Q_BLOCK)
    w2 = 2 * Q_BLOCK
    gk = ATT_GROUP * MOBA_BLOCK

    lane = lax.broadcasted_iota(jnp.int32, (1, w2), 1)
    tl = lane % Q_BLOCK
    sl = lax.broadcasted_iota(jnp.int32, (MOBA_BLOCK, w2), 0)
    n_iota = lax.broadcasted_iota(jnp.int32, (nblk, w2), 0)
    blk_off = (n_iota * MOBA_BLOCK - qi * Q_BLOCK).astype(F32)
    causal = half * Q_BLOCK + tl - sl >= 0
    z = jnp.zeros((HEAD_DIM, Q_BLOCK), F32)

    state = []
    for c in range(N_PAIRS):
        head = 2 * c + (lane >= Q_BLOCK).astype(jnp.int32)
        slope = jnp.exp2(-(head + 1).astype(F32)) * LOG2E

        @pl.when(qi == 0)
        def _():
            dmat_s[c] = slope * sl.astype(F32)

        qT = qT_ref[c * ATT_PAIR_W:(c + 1) * ATT_PAIR_W, :]
        qbd = jnp.concatenate(
            [jnp.concatenate([qT[:HEAD_DIM], z], axis=1), jnp.concatenate([z, qT[HEAD_DIM:]], axis=1)], axis=0)
        qs_s[c] = (qbd * (SCALE * LOG2E)).astype(BF16)

        gate = jnp.dot(km_ref[:, c * ATT_PAIR_W:(c + 1) * ATT_PAIR_W], qbd, precision=HIGHEST,
                       preferred_element_type=F32)
        gate = jnp.where(n_iota < own, gate, -jnp.inf)
        sel = jnp.logical_and(_topk_mask(gate, MOBA_TOPK, 0) > 0.5, n_iota < own)
        bias_s[c] = jnp.where(sel, slope * blk_off, NEG)

        k_own = k_ref[pl.ds(pl.multiple_of(own * MOBA_BLOCK, MOBA_BLOCK), MOBA_BLOCK),
                      c * ATT_PAIR_W:(c + 1) * ATT_PAIR_W]
        s0 = _dot(k_own, qs_s[c]) + dmat_s[c] - slope * (half * Q_BLOCK).astype(F32)
        s0 = jnp.where(causal, s0, NEG)
        m0 = jnp.max(s0, axis=0, keepdims=True)
        p0 = jnp.exp2(s0 - m0).astype(BF16)
        col0 = pl.multiple_of((own % ATT_GROUP) * MOBA_BLOCK, MOBA_BLOCK)
        res0 = _dot(vTg_ref[own // ATT_GROUP, c * ATT_SLAB:(c + 1) * ATT_SLAB, pl.ds(col0, MOBA_BLOCK)], p0)
        state.append((m0, res0[ATT_PAIR_W:ATT_PAIR_W + 1], res0[:ATT_PAIR_W]))

    def scores(j, c):
        out = []
        for g in range(ATT_GROUP):
            row0 = pl.multiple_of((j * ATT_GROUP + g) * MOBA_BLOCK, MOBA_BLOCK)
            out.append(_dot(k_ref[pl.ds(row0, MOBA_BLOCK), c * ATT_PAIR_W:(c + 1) * ATT_PAIR_W], qs_s[c]) + dmat_s[c])
        return out

    def body(j, carry):
        out = []
        ahead = [scores(j, 0), scores(j, 1)]
        for c in range(N_PAIRS):
            m, l, acc = carry[c]
            parts = ahead.pop(0)
            if c + 2 < N_PAIRS:
                ahead.append(scores(j, c + 2))
            biases = []
            m_new = m
            for g in range(ATT_GROUP):
                bg = bias_s[c, pl.ds(j * ATT_GROUP + g, 1), :]
                m_new = jnp.maximum(m_new, jnp.max(parts[g], axis=0, keepdims=True) + bg)
                biases.append(bg)
            a = jnp.exp2(m - m_new)
            pj = jnp.concatenate(
                [jnp.exp2(parts[g] - (m_new - biases[g])).astype(BF16) for g in range(ATT_GROUP)], axis=0)
            res = _dot(vTg_ref[j, c * ATT_SLAB:(c + 1) * ATT_SLAB, :], pj)
            out.append((m_new, a * l + res[ATT_PAIR_W:ATT_PAIR_W + 1], a * acc + res[:ATT_PAIR_W]))
        return tuple(out)

    n_trip = (own + (ATT_GROUP - 1)) // ATT_GROUP
    state = lax.fori_loop(0, n_trip, body, tuple(state))
    for c in range(N_PAIRS):
        _, l, acc = state[c]
        inv = 1.0 / l
        r0 = c * ATT_PAIR_W
        oT_ref[r0:r0 + HEAD_DIM, :] = acc[0:HEAD_DIM, 0:Q_BLOCK] * inv[:, 0:Q_BLOCK]
        oT_ref[r0 + HEAD_DIM:r0 + ATT_PAIR_W, :] = acc[HEAD_DIM:, Q_BLOCK:] * inv[:, Q_BLOCK:]


def _moba_prompt(qT, k_bf, vTg, kmean):
    w, s = qT.shape
    nblk = kmean.shape[0]
    resident = lambda a: pl.BlockSpec(a.shape, lambda i: (0,) * a.ndim, pipeline_mode=pl.Buffered(1))
    return pl.pallas_call(
        functools.partial(_moba_prompt_body, nblk=nblk),
        grid=(s // Q_BLOCK,),
        in_specs=[pl.BlockSpec((w, Q_BLOCK), lambda i: (0, i)), resident(k_bf), resident(vTg), resident(kmean)],
        out_specs=pl.BlockSpec((w, Q_BLOCK), lambda i: (0, i)),
        out_shape=jax.ShapeDtypeStruct((w, s), F32),
        scratch_shapes=[pltpu.VMEM((N_PAIRS, nblk, 2 * Q_BLOCK), F32),
                        pltpu.VMEM((N_PAIRS, MOBA_BLOCK, 2 * Q_BLOCK), F32),
                        pltpu.VMEM((N_PAIRS, ATT_PAIR_W, 2 * Q_BLOCK), BF16)],
        compiler_params=_cparams(("arbitrary",), LARGE_VMEM_LIMIT),
        name="moba_prompt",
    )(qT, k_bf, vTg, kmean)


def _softmax_rows(logits):
    m = jnp.max(logits, axis=-1, keepdims=True)
    e = jnp.exp(logits - m)
    return e / jnp.sum(e, axis=-1, keepdims=True)


def _mem_prompt_body(qm_ref, mkv_ref, o_ref):
    outs = []
    for h in range(MEM_HEADS):
        lo, hi = h * HEAD_DIM, (h + 1) * HEAD_DIM
        q = (qm_ref[:, lo:hi] * SCALE).astype(BF16)
        mk = mkv_ref[:, lo:hi].astype(BF16)
        mv = mkv_ref[:, MEM_W + lo:MEM_W + hi].astype(BF16)
        p = _softmax_rows(_dot_nt(q, mk))
        outs.append(_dot(p.astype(BF16), mv))
    o_ref[...] = jnp.concatenate(outs, axis=1)


def _mem_attend_prompt(qm, mkv):
    n, w = qm.shape
    tm = min(512, n)
    return pl.pallas_call(
        _mem_prompt_body,
        grid=(n // tm,),
        in_specs=[pl.BlockSpec((tm, w), lambda i: (i, 0)), pl.BlockSpec(mkv.shape, lambda i: (0, 0))],
        out_specs=pl.BlockSpec((tm, w), lambda i: (i, 0)),
        out_shape=jax.ShapeDtypeStruct((n, w), F32),
        compiler_params=_cparams(("parallel",)),
        name="mem_attend_prompt",
    )(qm, mkv)


def _mem_sample_body(qm_ref, mkT_ref, mvT_ref, o_ref, *, sb):
    t, w = qm_ref.shape[1:]
    n_mem = mkT_ref.shape[-1]
    lane_head = lax.broadcasted_iota(jnp.int32, (t, w), 1) // HEAD_DIM
    for i in range(sb):
        q = qm_ref[i] * SCALE
        qbd = jnp.concatenate([jnp.where(lane_head == h, q, 0.0) for h in range(MEM_HEADS)], axis=0).astype(BF16)
        p = _softmax_rows(_dot(qbd, mkT_ref[i].reshape(w, n_mem).astype(BF16)))
        o_all = _dot_nt(p.astype(BF16), mvT_ref[i].reshape(w, n_mem).astype(BF16))
        out = jnp.zeros((t, w), F32)
        for h in range(MEM_HEADS):
            out = out + jnp.where(lane_head == h, o_all[h * t:(h + 1) * t], 0.0)
        o_ref[i] = out


def _mem_attend_sample(qm3, mkT, mvT):
    b, t, w = qm3.shape
    sb = min(8, b)
    blk = (sb,) + mkT.shape[1:]
    return pl.pallas_call(
        functools.partial(_mem_sample_body, sb=sb),
        grid=(b // sb,),
        in_specs=[pl.BlockSpec((sb, t, w), lambda i: (i, 0, 0)),
                  pl.BlockSpec(blk, lambda i: (i, 0, 0, 0)), pl.BlockSpec(blk, lambda i: (i, 0, 0, 0))],
        out_specs=pl.BlockSpec((sb, t, w), lambda i: (i, 0, 0)),
        out_shape=jax.ShapeDtypeStruct((b, t, w), F32),
        compiler_params=_cparams(("parallel",)),
        name="mem_attend_sample",
    )(qm3, mkT, mvT)


def _mix_body(x_ref, oa_ref, u_ref, vn_ref, oc_ref, wm_ref, bz_ref, wo_ref, g_ref, b_ref, h_ref, *, tm, oa_transposed):
    lane_grp = lax.broadcasted_iota(jnp.int32, (SGU_CHUNK, SGU_W), 1) // HEAD_DIM
    zs = []
    for c in range(tm // SGU_CHUNK):
        vn = vn_ref[c * SGU_CHUNK:(c + 1) * SGU_CHUNK, :].astype(BF16)
        z = bz_ref[...]
        for g in range(SGU_GROUPS):
            z = z + jnp.where(lane_grp == g, _dot(wm_ref[g], vn), 0.0)
        zs.append(z)
    ob = u_ref[...] * jnp.concatenate(zs, axis=0)
    oa = oa_ref[...].T if oa_transposed else oa_ref[...]
    y = _dot(oa.astype(BF16), wo_ref[0:MOBA_W, :])
    y = y + _dot(ob.astype(BF16), wo_ref[MOBA_W:MOBA_W + SGU_W, :])
    y = y + _dot(oc_ref[...].astype(BF16), wo_ref[MOBA_W + SGU_W:, :])
    h_ref[...] = _layer_norm(DEEPNORM_ALPHA * x_ref[...] + y, g_ref[...], b_ref[...])


def _mix(x, oa, u, vn, oc, wmat_bf, bz, wo_bf, g, b, oa_transposed=False):
    n, d = x.shape
    tm = min(512, n)
    row = lambda a: pl.BlockSpec((tm, a.shape[1]), lambda i: (i, 0))
    full = lambda a: pl.BlockSpec(a.shape, lambda i: (0,) * a.ndim)
    oa_spec = pl.BlockSpec((oa.shape[0], tm), lambda i: (0, i)) if oa_transposed else row(oa)
    return pl.pallas_call(
        functools.partial(_mix_body, tm=tm, oa_transposed=oa_transposed),
        grid=(n // tm,),
        in_specs=[row(x), oa_spec, row(u), row(vn), row(oc), full(wmat_bf), full(bz), full(wo_bf), full(g), full(b)],
        out_specs=row(x),
        out_shape=jax.ShapeDtypeStruct((n, d), F32),
        compiler_params=_cparams(("parallel",)),
        name="mix",
    )(x, oa, u, vn, oc, wmat_bf, bz, wo_bf, g, b)


def _rank_rows(x, n_rows, axis):
    idx = lax.broadcasted_iota(jnp.int32, x.shape, axis)
    rank = jnp.zeros(x.shape, F32)
    for j in range(n_rows):
        xj = lax.slice_in_dim(x, j, j + 1, axis=axis)
        tie = jnp.where(idx > j, 1.0, 0.0)
        rank = rank + jnp.where(xj > x, 1.0, jnp.where(xj == x, tie, 0.0))
    return rank


def _router_body(h_ref, wrT_ref, rb_ref, o_ref, *, n_exp):
    tm = h_ref.shape[0]
    gsz = n_exp // N_EXPERT_GROUPS
    scores = _sigmoid(_dot_nt(wrT_ref[...], h_ref[...], precision=HIGHEST))
    biased = scores + rb_ref[...]
    b3 = biased.reshape(N_EXPERT_GROUPS, gsz, tm)
    top2 = jnp.sum(jnp.where(_rank_rows(b3, gsz, 1) < 2.0, b3, 0.0), axis=1, keepdims=True)
    gs = jnp.broadcast_to(top2, b3.shape).reshape(n_exp, tm)
    grp = lax.broadcasted_iota(jnp.int32, (n_exp, tm), 0) // gsz
    grank = jnp.zeros((n_exp, tm), F32)
    for g2 in range(N_EXPERT_GROUPS):
        xg = gs[g2 * gsz:g2 * gsz + 1, :]
        tie = jnp.where(grp > g2, 1.0, 0.0)
        grank = grank + jnp.where(xg > gs, 1.0, jnp.where(xg == gs, tie, 0.0))
    masked = jnp.where(grank < float(TOPK_GROUPS), biased, -jnp.inf)
    w = _topk_mask(masked, TOP_K, 0) * scores
    o_ref[...] = w / jnp.sum(w, axis=0, keepdims=True) * ROUTE_SCALE


def _router(h, wrT, rb_col):
    n, d = h.shape
    n_exp = wrT.shape[0]
    tm = min(1024, n)
    return pl.pallas_call(
        functools.partial(_router_body, n_exp=n_exp),
        grid=(n // tm,),
        in_specs=[pl.BlockSpec((tm, d), lambda i: (i, 0)), pl.BlockSpec(wrT.shape, lambda i: (0, 0)),
                  pl.BlockSpec(rb_col.shape, lambda i: (0, 0))],
        out_specs=pl.BlockSpec((n_exp, tm), lambda i: (0, i)),
        out_shape=jax.ShapeDtypeStruct((n_exp, n), F32),
        compiler_params=_cparams(("parallel",)),
        name="router",
    )(h, wrT, rb_col)


def _moe_body(h_ref, gt_ref, wg_ref, wu_ref, wd_ref, sg_ref, su_ref, sd_ref, g_ref, b_ref, o_ref, acc_ref, *, eps):
    s = pl.program_id(1)
    hb = h_ref[...].astype(BF16)

    @pl.when(s == 0)
    def _():
        hs = _silu(_dot(hb, sg_ref[...])) * _dot(hb, su_ref[...])
        acc_ref[...] = _dot(hs.astype(BF16), sd_ref[...])

    lane = lax.broadcasted_iota(jnp.int32, gt_ref.shape, 1)
    hids = []
    for j in range(eps):
        gcol = jnp.sum(jnp.where(lane == s * eps + j, gt_ref[...], 0.0), axis=1, keepdims=True)
        hid = _silu(_dot(hb, wg_ref[j].astype(BF16))) * _dot(hb, wu_ref[j].astype(BF16)) * gcol
        hids.append(hid.astype(BF16))
    wd = wd_ref[...].reshape(eps * wd_ref.shape[1], wd_ref.shape[2]).astype(BF16)
    acc_ref[...] += _dot(jnp.concatenate(hids, axis=1), wd)

    @pl.when(s == pl.num_programs(1) - 1)
    def _():
        o_ref[...] = _layer_norm(DEEPNORM_ALPHA * h_ref[...] + acc_ref[...], g_ref[...], b_ref[...])


def _moe(h, gates, wg, wu, wd, sg, su, sd, g, b):
    n, d = h.shape
    n_exp, _, f = wg.shape
    tm = min(1024, n)
    eps = MOE_EXPERTS_PER_STEP
    full = lambda a: pl.BlockSpec(a.shape, lambda i, e: (0,) * a.ndim)
    return pl.pallas_call(
        functools.partial(_moe_body, eps=eps),
        grid=(n // tm, n_exp // eps),
        in_specs=[pl.BlockSpec((tm, d), lambda i, e: (i, 0)), pl.BlockSpec((tm, n_exp), lambda i, e: (i, 0)),
                  pl.BlockSpec((eps, d, f), lambda i, e: (e, 0, 0)), pl.BlockSpec((eps, d, f), lambda i, e: (e, 0, 0)),
                  pl.BlockSpec((eps, f, d), lambda i, e: (e, 0, 0)),
                  full(sg), full(su), full(sd), full(g), full(b)],
        out_specs=pl.BlockSpec((tm, d), lambda i, e: (i, 0)),
        out_shape=jax.ShapeDtypeStruct((n, d), F32),
        scratch_shapes=[pltpu.VMEM((tm, d), F32)],
        compiler_params=_cparams(("parallel", "arbitrary"), LARGE_VMEM_LIMIT),
        name="moe",
    )(h, gates, wg, wu, wd, sg, su, sd, g, b)


def _head_block_diag(q):
    lane_head = lax.broadcasted_iota(jnp.int32, q.shape, 1) // HEAD_DIM
    return jnp.concatenate([jnp.where(lane_head == h, q, 0.0) for h in range(MOBA_HEADS)], axis=0)


def _sample_keys_body(pt_ref, q_ref, *refs, ppstep):
    pages, sc_ref = refs[:ppstep], refs[ppstep]
    qbd = (_head_block_diag(q_ref[0]) * SCALE).astype(BF16)
    for i in range(ppstep):
        kt = pages[i][0].reshape(MOBA_W, PAGE_SIZE)
        sc_ref[0, :, i * PAGE_SIZE:(i + 1) * PAGE_SIZE] = _dot(qbd, kt.astype(BF16))


def _page_spec(n_pages, ppstep, i):
    return pl.BlockSpec((1, MOBA_HEADS, HEAD_DIM, PAGE_SIZE),
                        lambda b, s, pt: (pt[b * n_pages + s * ppstep + i], 0, 0, 0))


def _sample_keys(pt_flat, q3, ckT, n_pages):
    b, t, w = q3.shape
    ppstep = min(SAMPLE_PAGES_PER_STEP, n_pages)
    ht = MOBA_HEADS * t
    grid_spec = pltpu.PrefetchScalarGridSpec(
        num_scalar_prefetch=1,
        grid=(b, n_pages // ppstep),
        in_specs=[pl.BlockSpec((1, t, w), lambda b_, s, pt: (b_, 0, 0))]
        + [_page_spec(n_pages, ppstep, i) for i in range(ppstep)],
        out_specs=pl.BlockSpec((1, ht, ppstep * PAGE_SIZE), lambda b_, s, pt: (b_, 0, s)),
    )
    return pl.pallas_call(
        functools.partial(_sample_keys_body, ppstep=ppstep),
        grid_spec=grid_spec,
        out_shape=jax.ShapeDtypeStruct((b, ht, n_pages * PAGE_SIZE), F32),
        compiler_params=_cparams(("arbitrary", "arbitrary"), LARGE_VMEM_LIMIT),
        name="moba_sample_keys",
    )(pt_flat, q3, *([ckT] * ppstep))


def _sample_values_body(pt_ref, sc_ref, q_ref, kn_ref, vn_ref, *refs, ppstep, n_pages, t_new):
    pages, o_ref, (lg_s, p_s, l_s, acc_s) = refs[:ppstep], refs[ppstep], refs[ppstep + 1:]
    s = pl.program_id(1)
    ppb = MOBA_BLOCK // PAGE_SIZE
    n_fp = n_pages // ppb
    n_sel = min(MOBA_TOPK, n_fp)
    past = n_pages * PAGE_SIZE
    ht = MOBA_HEADS * t_new
    bps = ppstep // ppb

    @pl.when(s == 0)
    def _():
        qbd = _head_block_diag(q_ref[0])
        blane = lax.broadcasted_iota(jnp.int32, (ht, LANES), 1)
        gate = jnp.full((ht, LANES), -jnp.inf, F32)
        for n in range(n_fp):
            gsum = jnp.sum(sc_ref[0, :, n * MOBA_BLOCK:(n + 1) * MOBA_BLOCK], axis=1, keepdims=True)
            gate = jnp.where(blane == n, gsum, gate)
        sel = _topk_mask(gate, n_sel, 1)

        row = lax.broadcasted_iota(jnp.int32, (ht, 1), 0)
        slope = jnp.exp2(-((row // t_new) + 1).astype(F32))
        t_col = row % t_new
        qs = (qbd * SCALE).astype(BF16)
        pad = jnp.zeros((LANES - t_new, MOBA_W), F32)
        knew = jnp.concatenate([kn_ref[0], pad], axis=0).astype(BF16)
        vnew = jnp.concatenate([vn_ref[0], pad], axis=0).astype(BF16)
        l_own = _dot_nt(qs, knew) - slope * (t_col - blane).astype(F32)
        l_own = jnp.where(blane <= t_col, l_own, NEG)
        pos = lax.broadcasted_iota(jnp.int32, (ht, MOBA_BLOCK), 1)
        base = slope * (t_col - pos).astype(F32)
        m_el = jnp.full((ht, LANES), NEG, F32)
        for n in range(n_fp):
            cols = slice(n * MOBA_BLOCK, (n + 1) * MOBA_BLOCK)
            ln = sc_ref[0, :, cols] - base - slope * float(past - n * MOBA_BLOCK)
            ln = jnp.where(sel[:, n:n + 1] > 0.5, ln, NEG)
            m_el = jnp.maximum(m_el, jnp.maximum(ln[:, :LANES], ln[:, LANES:]))
            lg_s[:, cols] = ln
        m = jnp.maximum(jnp.max(l_own, axis=1, keepdims=True), jnp.max(m_el, axis=1, keepdims=True))
        p_own = jnp.exp(l_own - m)
        s_el = jnp.zeros((ht, LANES), F32)
        for n in range(n_fp):
            pn = jnp.exp(lg_s[:, n * MOBA_BLOCK:(n + 1) * MOBA_BLOCK] - m)
            s_el = s_el + (pn[:, :LANES] + pn[:, LANES:])
            p_s[n // bps, :, (n % bps) * MOBA_BLOCK:(n % bps + 1) * MOBA_BLOCK] = pn.astype(BF16)
        lsum = jnp.sum(p_own, axis=1, keepdims=True) + jnp.sum(s_el, axis=1, keepdims=True)
        l_s[...] = jnp.broadcast_to(lsum, (ht, LANES))
        acc_s[...] = _dot(p_own.astype(BF16), vnew)

    vt = jnp.concatenate([pages[i][0].reshape(MOBA_W, PAGE_SIZE).astype(BF16) for i in range(ppstep)], axis=1)
    acc_s[...] += _dot_nt(p_s[s], vt)

    @pl.when(s == pl.num_programs(1) - 1)
    def _():
        o_all = acc_s[...] / l_s[:, 0:1]
        lane_head = lax.broadcasted_iota(jnp.int32, (t_new, MOBA_W), 1) // HEAD_DIM
        out = jnp.zeros((t_new, MOBA_W), F32)
        for h in range(MOBA_HEADS):
            out = out + jnp.where(lane_head == h, o_all[h * t_new:(h + 1) * t_new], 0.0)
        o_ref[0] = out


def _sample_values(pt_flat, scores, q3, kn3, vn3, cvT, n_pages):
    b, t, w = q3.shape
    ppstep = min(SAMPLE_PAGES_PER_STEP, n_pages)
    ht = MOBA_HEADS * t
    n_steps = n_pages // ppstep
    small = lambda a: pl.BlockSpec((1,) + a.shape[1:], lambda b_, s, pt: (b_,) + (0,) * (a.ndim - 1))
    grid_spec = pltpu.PrefetchScalarGridSpec(
        num_scalar_prefetch=1,
        grid=(b, n_steps),
        in_specs=[small(scores), small(q3), small(kn3), small(vn3)]
        + [_page_spec(n_pages, ppstep, i) for i in range(ppstep)],
        out_specs=pl.BlockSpec((1, t, w), lambda b_, s, pt: (b_, 0, 0)),
        scratch_shapes=[pltpu.VMEM((ht, n_pages * PAGE_SIZE), F32),
                        pltpu.VMEM((n_steps, ht, ppstep * PAGE_SIZE), BF16),
                        pltpu.VMEM((ht, LANES), F32),
                        pltpu.VMEM((ht, w), F32)],
    )
    return pl.pallas_call(
        functools.partial(_sample_values_body, ppstep=ppstep, n_pages=n_pages, t_new=t),
        grid_spec=grid_spec,
        out_shape=jax.ShapeDtypeStruct((b, t, w), F32),
        compiler_params=_cparams(("arbitrary", "arbitrary"), LARGE_VMEM_LIMIT),
        name="moba_sample_values",
    )(pt_flat, scores, q3, kn3, vn3, *([cvT] * ppstep))


def _sgu_block_diag(w_s, t_new):
    g = w_s.shape[0]
    reps = SGU_CHUNK // t_new
    small = jnp.tril(w_s)[:, :t_new, :t_new]
    eye = jnp.eye(reps, dtype=w_s.dtype)
    return jnp.einsum("rq,gts->grtqs", eye, small).reshape(g, SGU_CHUNK, SGU_CHUNK)


def _sgu_bias(b_s, rows):
    return jnp.repeat(b_s[:, :rows].T, HEAD_DIM, axis=1)


def _ffn(h1, wrT, rb_col, moe_w, g, b):
    gates = _router(h1, wrT, rb_col).T
    return _moe(h1, gates, *moe_w, g, b)


def kernel(x_prompt, x_sample, mem_prompt, cache_k, cache_v, page_table, cache_mem_k, cache_mem_v, w_in, w_out, w_mem_kv, sgu_ln_g, sgu_ln_b, sgu_w_s, sgu_b_s, ln1_g, ln1_b, w_router, router_bias, w_gate, w_up, w_down, ws_gate, ws_up, ws_down, ln2_g, ln2_b):
    assert w_in.shape[0] == DEPTH and x_prompt.shape[0] == 1
    _, seq, d_model = x_prompt.shape
    db, t_new, _ = x_sample.shape
    n_pages = page_table.shape[1]
    assert n_pages % (MOBA_BLOCK // PAGE_SIZE) == 0 and SGU_CHUNK % t_new == 0
    n_s = db * t_new
    assert n_s % SGU_CHUNK == 0 and seq % (8 * MOBA_BLOCK) == 0 and seq % (ATT_GROUP * MOBA_BLOCK) == 0
    assert n_pages % min(SAMPLE_PAGES_PER_STEP, n_pages) == 0

    w_in_bf = w_in[0].astype(BF16)
    wo_bf = w_out[0].astype(BF16)
    wmem_bf = w_mem_kv[0].astype(BF16)
    sg_g, sg_b = sgu_ln_g, sgu_ln_b
    moe_w = (w_gate[0], w_up[0], w_down[0],
             ws_gate[0].astype(BF16), ws_up[0].astype(BF16), ws_down[0].astype(BF16))
    wrT = jnp.swapaxes(w_router[0], 0, 1)
    rb_col = router_bias[0][:, None]

    xp = x_prompt[0]
    qT, k, k_bf, kTp, vTp, vTg, u, svn, qm = _in_proj_prompt(xp, w_in_bf, sg_g, sg_b)
    mkv = _mem_kv(mem_prompt[0], wmem_bf)
    oT = _moba_prompt(qT, k_bf, vTg, _k_block_mean(k))
    o_c = _mem_attend_prompt(qm, mkv)
    wm_p = jnp.tril(sgu_w_s[0]).astype(BF16)
    h1 = _mix(xp, oT, u, svn, o_c, wm_p, _sgu_bias(sgu_b_s[0], SGU_CHUNK), wo_bf, ln1_g, ln1_b, oa_transposed=True)
    y_prompt = _ffn(h1, wrT, rb_col, moe_w, ln2_g, ln2_b)[None]

    n_pg = seq // PAGE_SIZE
    to_pages = lambda zT: jnp.swapaxes(zT.reshape(1, 1, n_pg, MOBA_HEADS, HEAD_DIM, PAGE_SIZE), -1, -2)
    k_prompt, v_prompt = to_pages(kTp), to_pages(vTp)
    n_mem = mkv.shape[0]
    mem_k_prompt = mkv[:, :MEM_W].reshape(1, 1, n_mem, MEM_HEADS, HEAD_DIM)
    mem_v_prompt = mkv[:, MEM_W:].reshape(1, 1, n_mem, MEM_HEADS, HEAD_DIM)

    xs = x_sample.reshape(n_s, d_model)
    q, k, v, u, svn, qm = _in_proj(xs, w_in_bf, sg_g, sg_b)
    heads = lambda z: z.reshape(db, t_new, MOBA_HEADS, HEAD_DIM).transpose(0, 2, 1, 3)
    k4, v4 = heads(k), heads(v)
    ckT = jnp.swapaxes(cache_k[0], -1, -2)
    cvT = jnp.swapaxes(cache_v[0], -1, -2)
    pt_flat = page_table.reshape(-1)
    three = lambda z: z.reshape(db, t_new, MOBA_W)
    scores = _sample_keys(pt_flat, three(q), ckT, n_pages)
    o_a = _sample_values(pt_flat, scores, three(q), three(k), three(v), cvT, n_pages).reshape(n_s, MOBA_W)
    mkT = cache_mem_k[0].transpose(0, 2, 3, 1)
    mvT = cache_mem_v[0].transpose(0, 2, 3, 1)
    o_c = _mem_attend_sample(qm.reshape(db, t_new, MEM_W), mkT, mvT).reshape(n_s, MEM_W)
    wm_s = _sgu_block_diag(sgu_w_s[0], t_new).astype(BF16)
    bz_s = jnp.tile(_sgu_bias(sgu_b_s[0], t_new), (SGU_CHUNK // t_new, 1))
    h1 = _mix(xs, o_a, u, svn, o_c, wm_s, bz_s, wo_bf, ln1_g, ln1_b)
    y_sample = _ffn(h1, wrT, rb_col, moe_w, ln2_g, ln2_b).reshape(db, t_new, d_model)

    k_sample, v_sample = k4[None], v4[None]
    sgu_v_sample = svn.reshape(1, db, t_new, SGU_W)
    return (y_prompt, y_sample, k_prompt, v_prompt, k_sample, v_sample, mem_k_prompt, mem_v_prompt, sgu_v_sample)
```

```python
import functools

import jax
import jax.numpy as jnp
from jax import lax
from jax.experimental import pallas as pl
from jax.experimental.pallas import tpu as pltpu

F32 = jnp.float32
BF16 = jnp.bfloat16

HEAD_DIM = 64
MOBA_HEADS = 8
SGU_GROUPS = 4
MEM_HEADS = 4
MOBA_W = MOBA_HEADS * HEAD_DIM
SGU_W = SGU_GROUPS * HEAD_DIM
MEM_W = MEM_HEADS * HEAD_DIM
MOBA_BLOCK = 256
MOBA_TOPK = 3
Q_BLOCK = 128
SGU_CHUNK = 128
PAGE_SIZE = 128
N_EXPERT_GROUPS = 8
TOPK_GROUPS = 4
TOP_K = 8
ROUTE_SCALE = 2.5
LN_EPS = 1e-5
NEG = -1e30
DEPTH = 1
DEEPNORM_ALPHA = (2 * DEPTH) ** 0.25
SCALE = HEAD_DIM ** -0.5
LOG2E = 1.4426950408889634
ATT_GROUP = 8
SAMPLE_PAGES_PER_STEP = 64
MOE_EXPERTS_PER_STEP = 4
LANES = 128
VMEM_LIMIT = 48 * 1024 * 1024
LARGE_VMEM_LIMIT = 58 * 1024 * 1024
HIGHEST = lax.Precision.HIGHEST


def _cparams(sem, limit=VMEM_LIMIT):
    return pltpu.CompilerParams(dimension_semantics=sem, vmem_limit_bytes=limit)


def _gelu(x):
    return x * (0.5 * (1.0 + jnp.tanh(0.7978845608028654 * (x + 0.044715 * (x * x * x)))))


def _sigmoid(x):
    return 1.0 / (1.0 + jnp.exp(-x))


def _silu(x):
    return x * _sigmoid(x)


def _layer_norm(x, g, b):
    mu = jnp.mean(x, axis=-1, keepdims=True)
    xc = x - mu
    var = jnp.mean(xc * xc, axis=-1, keepdims=True)
    return xc * lax.rsqrt(var + LN_EPS) * g + b


def _topk_mask(x, k, axis):
    n = x.shape[axis]
    idx = lax.broadcasted_iota(jnp.int32, x.shape, axis).astype(F32)
    sel = jnp.zeros(x.shape, F32)
    for _ in range(k):
        mx = jnp.max(x, axis=axis, keepdims=True)
        first = jnp.min(jnp.where(x == mx, idx, float(n)), axis=axis, keepdims=True)
        hit = idx == first
        sel = jnp.where(hit, 1.0, sel)
        x = jnp.where(hit, -jnp.inf, x)
    return sel


def _dot(a, b):
    return jnp.dot(a, b, preferred_element_type=F32)


def _dot_nt(a, b, precision=None):
    return lax.dot_general(a, b, (((1,), (1,)), ((), ())), precision=precision, preferred_element_type=F32)


def _project(x_ref, w_ref, g_ref, b_ref):
    x = x_ref[...].astype(BF16)
    c0, c1, c2, c3, c4 = MOBA_W, 2 * MOBA_W, 3 * MOBA_W, 3 * MOBA_W + SGU_W, 3 * MOBA_W + 2 * SGU_W
    q = _dot(x, w_ref[:, 0:c0])
    k = _dot(x, w_ref[:, c0:c1])
    v = _dot(x, w_ref[:, c1:c2])
    u = _gelu(_dot(x, w_ref[:, c2:c3]))
    sv = _layer_norm(_gelu(_dot(x, w_ref[:, c3:c4])), g_ref[...], b_ref[...])
    qm = _dot(x, w_ref[:, c4:c4 + MEM_W])
    return q, k, v, u, sv, qm


def _inproj_body(x_ref, w_ref, g_ref, b_ref, *out_refs):
    for ref, val in zip(out_refs, _project(x_ref, w_ref, g_ref, b_ref), strict=True):
        ref[...] = val


def _inproj_prompt_body(x_ref, w_ref, g_ref, b_ref, qT_ref, k_ref, kbf_ref, kTp_ref, vTp_ref, vTg_ref,
                        u_ref, sv_ref, qm_ref):
    q, k, v, u, sv, qm = _project(x_ref, w_ref, g_ref, b_ref)
    tm = q.shape[0]
    qT_ref[...] = q.T
    k_ref[...] = k
    kbf_ref[...] = k.astype(BF16)
    kT, vT = k.T, v.T
    for p in range(tm // PAGE_SIZE):
        kTp_ref[p] = kT[:, p * PAGE_SIZE:(p + 1) * PAGE_SIZE]
        vTp_ref[p] = vT[:, p * PAGE_SIZE:(p + 1) * PAGE_SIZE]
    vT_bf = vT.astype(BF16)
    ones = jnp.ones((ATT_ONES, tm), BF16)
    for c in range(N_PAIRS):
        vTg_ref[0, c * ATT_SLAB:c * ATT_SLAB + ATT_PAIR_W, :] = vT_bf[c * ATT_PAIR_W:(c + 1) * ATT_PAIR_W]
        vTg_ref[0, c * ATT_SLAB + ATT_PAIR_W:(c + 1) * ATT_SLAB, :] = ones
    u_ref[...] = u
    sv_ref[...] = sv
    qm_ref[...] = qm.T


def _in_proj_prompt(x, w_in_bf, sgu_g, sgu_b):
    n, d = x.shape
    tm = 512
    gk = ATT_GROUP * MOBA_BLOCK
    per_grp = gk // tm
    ppt = tm // PAGE_SIZE
    row = lambda w: pl.BlockSpec((tm, w), lambda i: (i, 0))
    full = lambda a: pl.BlockSpec(a.shape, lambda i: (0,) * a.ndim)
    pages = pl.BlockSpec((ppt, MOBA_W, PAGE_SIZE), lambda i: (i, 0, 0))
    sds = jax.ShapeDtypeStruct
    return pl.pallas_call(
        _inproj_prompt_body,
        grid=(n // tm,),
        in_specs=[row(d), full(w_in_bf), full(sgu_g), full(sgu_b)],
        out_specs=[pl.BlockSpec((MOBA_W, tm), lambda i: (0, i)), row(MOBA_W), row(MOBA_W), pages, pages,
                   pl.BlockSpec((1, N_PAIRS * ATT_SLAB, tm), lambda i: (i // per_grp, 0, i % per_grp)),
                   row(SGU_W), row(SGU_W), pl.BlockSpec((MEM_W, tm), lambda i: (0, i))],
        out_shape=[sds((MOBA_W, n), F32), sds((n, MOBA_W), F32), sds((n, MOBA_W), BF16),
                   sds((n // PAGE_SIZE, MOBA_W, PAGE_SIZE), F32), sds((n // PAGE_SIZE, MOBA_W, PAGE_SIZE), F32),
                   sds((n // gk, N_PAIRS * ATT_SLAB, gk), BF16),
                   sds((n, SGU_W), F32), sds((n, SGU_W), F32), sds((MEM_W, n), F32)],
        compiler_params=_cparams(("parallel",)),
        name="in_proj_prompt",
    )(x, w_in_bf, sgu_g, sgu_b)


def _in_proj(x, w_in_bf, sgu_g, sgu_b):
    n, d = x.shape
    tm = min(512, n)
    in_w = w_in_bf.shape[1]
    row = lambda w: pl.BlockSpec((tm, w), lambda i: (i, 0))
    full = lambda a: pl.BlockSpec(a.shape, lambda i: (0,) * a.ndim)
    widths = (MOBA_W, MOBA_W, MOBA_W, SGU_W, SGU_W, MEM_W)
    return pl.pallas_call(
        _inproj_body,
        grid=(n // tm,),
        in_specs=[row(d), full(w_in_bf), full(sgu_g), full(sgu_b)],
        out_specs=[row(w) for w in widths],
        out_shape=[jax.ShapeDtypeStruct((n, w), F32) for w in widths],
        compiler_params=_cparams(("parallel",)),
        name="in_proj",
    )(x, w_in_bf, sgu_g, sgu_b)


def _memkv_body(m_ref, w_ref, o_ref, oT_ref):
    kv = _dot(m_ref[...].astype(BF16), w_ref[...])
    o_ref[...] = kv
    oT_ref[...] = kv.T


def _mem_kv(mem, w_bf):
    n, _ = mem.shape
    w = w_bf.shape[1]
    return pl.pallas_call(
        _memkv_body,
        out_shape=[jax.ShapeDtypeStruct((n, w), F32), jax.ShapeDtypeStruct((w, n), F32)],
        compiler_params=pltpu.CompilerParams(vmem_limit_bytes=VMEM_LIMIT),
        name="mem_kv",
    )(mem, w_bf)


def _kmean_body(k_ref, o_ref):
    kb = k_ref[...].reshape(8, MOBA_BLOCK, k_ref.shape[1])
    o_ref[...] = jnp.sum(kb, axis=1) * (1.0 / MOBA_BLOCK)


def _k_block_mean(k):
    s, w = k.shape
    nblk = s // MOBA_BLOCK
    return pl.pallas_call(
        _kmean_body,
        grid=(nblk // 8,),
        in_specs=[pl.BlockSpec((8 * MOBA_BLOCK, w), lambda i: (i, 0))],
        out_specs=pl.BlockSpec((8, w), lambda i: (i, 0)),
        out_shape=jax.ShapeDtypeStruct((nblk, w), F32),
        compiler_params=_cparams(("parallel",)),
        name="k_block_mean",
    )(k)


ATT_PAIR_W = 2 * HEAD_DIM
ATT_ONES = 16
ATT_SLAB = ATT_PAIR_W + ATT_ONES
N_PAIRS = MOBA_HEADS // 2


def _moba_prompt_body(qT_ref, k_ref, vTg_ref, km_ref, oT_ref, bias_s, dmat_s, qs_s, *, nblk):
    qi = pl.program_id(0)
    own = qi // (MOBA_BLOCK // Q_BLOCK)
    half = qi % (MOBA_BLOCK // Q_BLOCK)
    w2 = 2 * Q_BLOCK
    gk = ATT_GROUP * MOBA_BLOCK

    lane = lax.broadcasted_iota(jnp.int32, (1, w2), 1)
    tl = lane % Q_BLOCK
    sl = lax.broadcasted_iota(jnp.int32, (MOBA_BLOCK, w2), 0)
    n_iota = lax.broadcasted_iota(jnp.int32, (nblk, w2), 0)
    blk_off = (n_iota * MOBA_BLOCK - qi * Q_BLOCK).astype(F32)
    causal = half * Q_BLOCK + tl - sl >= 0
    z = jnp.zeros((HEAD_DIM, Q_BLOCK), F32)

    state = []
    for c in range(N_PAIRS):
        head = 2 * c + (lane >= Q_BLOCK).astype(jnp.int32)
        slope = jnp.exp2(-(head + 1).astype(F32)) * LOG2E

        @pl.when(qi == 0)
        def _():
            dmat_s[c] = slope * sl.astype(F32)

        qT = qT_ref[c * ATT_PAIR_W:(c + 1) * ATT_PAIR_W, :]
        qbd = jnp.concatenate(
            [jnp.concatenate([qT[:HEAD_DIM], z], axis=1), jnp.concatenate([z, qT[HEAD_DIM:]], axis=1)], axis=0)
        qs_s[c] = (qbd * (SCALE * LOG2E)).astype(BF16)

        gate = jnp.dot(km_ref[:, c * ATT_PAIR_W:(c + 1) * ATT_PAIR_W], qbd, precision=HIGHEST,
                       preferred_element_type=F32)
        gate = jnp.where(n_iota < own, gate, -jnp.inf)
        sel = jnp.logical_and(_topk_mask(gate, MOBA_TOPK, 0) > 0.5, n_iota < own)
        bias_s[c] = jnp.where(sel, slope * blk_off, NEG)

        k_own = k_ref[pl.ds(pl.multiple_of(own * MOBA_BLOCK, MOBA_BLOCK), MOBA_BLOCK),
                      c * ATT_PAIR_W:(c + 1) * ATT_PAIR_W]
        s0 = _dot(k_own, qs_s[c]) + dmat_s[c] - slope * (half * Q_BLOCK).astype(F32)
        s0 = jnp.where(causal, s0, NEG)
        m0 = jnp.max(s0, axis=0, keepdims=True)
        p0 = jnp.exp2(s0 - m0).astype(BF16)
        col0 = pl.multiple_of((own % ATT_GROUP) * MOBA_BLOCK, MOBA_BLOCK)
        res0 = _dot(vTg_ref[own // ATT_GROUP, c * ATT_SLAB:(c + 1) * ATT_SLAB, pl.ds(col0, MOBA_BLOCK)], p0)
        state.append((m0, res0[ATT_PAIR_W:ATT_PAIR_W + 1], res0[:ATT_PAIR_W]))

    def scores(j, c):
        out = []
        for g in range(ATT_GROUP):
            row0 = pl.multiple_of((j * ATT_GROUP + g) * MOBA_BLOCK, MOBA_BLOCK)
            out.append(_dot(k_ref[pl.ds(row0, MOBA_BLOCK), c * ATT_PAIR_W:(c + 1) * ATT_PAIR_W], qs_s[c]) + dmat_s[c])
        return out

    def body(j, carry):
        out = []
        ahead = [scores(j, 0), scores(j, 1)]
        for c in range(N_PAIRS):
            m, l, acc = carry[c]
            parts = ahead.pop(0)
            if c + 2 < N_PAIRS:
                ahead.append(scores(j, c + 2))
            biases = []
            m_new = m
            for g in range(ATT_GROUP):
                bg = bias_s[c, pl.ds(j * ATT_GROUP + g, 1), :]
                m_new = jnp.maximum(m_new, jnp.max(parts[g], axis=0, keepdims=True) + bg)
                biases.append(bg)
            a = jnp.exp2(m - m_new)
            pj = jnp.concatenate(
                [jnp.exp2(parts[g] - (m_new - biases[g])).astype(BF16) for g in range(ATT_GROUP)], axis=0)
            res = _dot(vTg_ref[j, c * ATT_SLAB:(c + 1) * ATT_SLAB, :], pj)
            out.append((m_new, a * l + res[ATT_PAIR_W:ATT_PAIR_W + 1], a * acc + res[:ATT_PAIR_W]))
        return tuple(out)

    n_trip = (own + (ATT_GROUP - 1)) // ATT_GROUP
    state = lax.fori_loop(0, n_trip, body, tuple(state))
    for c in range(N_PAIRS):
        _, l, acc = state[c]
        inv = 1.0 / l
        r0 = c * ATT_PAIR_W
        oT_ref[r0:r0 + HEAD_DIM, :] = acc[0:HEAD_DIM, 0:Q_BLOCK] * inv[:, 0:Q_BLOCK]
        oT_ref[r0 + HEAD_DIM:r0 + ATT_PAIR_W, :] = acc[HEAD_DIM:, Q_BLOCK:] * inv[:, Q_BLOCK:]


def _moba_prompt(qT, k_bf, vTg, kmean):
    w, s = qT.shape
    nblk = kmean.shape[0]
    resident = lambda a: pl.BlockSpec(a.shape, lambda i: (0,) * a.ndim, pipeline_mode=pl.Buffered(1))
    return pl.pallas_call(
        functools.partial(_moba_prompt_body, nblk=nblk),
        grid=(s // Q_BLOCK,),
        in_specs=[pl.BlockSpec((w, Q_BLOCK), lambda i: (0, i)), resident(k_bf), resident(vTg), resident(kmean)],
        out_specs=pl.BlockSpec((w, Q_BLOCK), lambda i: (0, i)),
        out_shape=jax.ShapeDtypeStruct((w, s), F32),
        scratch_shapes=[pltpu.VMEM((N_PAIRS, nblk, 2 * Q_BLOCK), F32),
                        pltpu.VMEM((N_PAIRS, MOBA_BLOCK, 2 * Q_BLOCK), F32),
                        pltpu.VMEM((N_PAIRS, ATT_PAIR_W, 2 * Q_BLOCK), BF16)],
        compiler_params=_cparams(("arbitrary",), LARGE_VMEM_LIMIT),
        name="moba_prompt",
    )(qT, k_bf, vTg, kmean)


def _softmax_rows(logits):
    m = jnp.max(logits, axis=-1, keepdims=True)
    e = jnp.exp(logits - m)
    return e / jnp.sum(e, axis=-1, keepdims=True)


def _mem_prompt_body(qmT_ref, mkv_ref, mkvT_ref, oT_ref):
    for h in range(MEM_HEADS):
        lo, hi = h * HEAD_DIM, (h + 1) * HEAD_DIM
        qT = (qmT_ref[lo:hi, :] * SCALE).astype(BF16)
        lg = _dot(mkv_ref[:, lo:hi].astype(BF16), qT)
        e = jnp.exp(lg - jnp.max(lg, axis=0, keepdims=True))
        den = jnp.sum(e, axis=0, keepdims=True)
        oT_ref[lo:hi, :] = _dot(mkvT_ref[MEM_W + lo:MEM_W + hi, :].astype(BF16), e.astype(BF16)) / den


def _mem_attend_prompt(qmT, mkv, mkvT):
    w, n = qmT.shape
    tm = min(512, n)
    return pl.pallas_call(
        _mem_prompt_body,
        grid=(n // tm,),
        in_specs=[pl.BlockSpec((w, tm), lambda i: (0, i)), pl.BlockSpec(mkv.shape, lambda i: (0, 0)),
                  pl.BlockSpec(mkvT.shape, lambda i: (0, 0))],
        out_specs=pl.BlockSpec((w, tm), lambda i: (0, i)),
        out_shape=jax.ShapeDtypeStruct((w, n), F32),
        compiler_params=_cparams(("parallel",)),
        name="mem_attend_prompt",
    )(qmT, mkv, mkvT)


def _mem_sample_body(qm_ref, mkT_ref, mvT_ref, o_ref, *, sb):
    t, w = qm_ref.shape[1:]
    n_mem = mkT_ref.shape[-1]
    lane_head = lax.broadcasted_iota(jnp.int32, (t, w), 1) // HEAD_DIM
    for i in range(sb):
        q = qm_ref[i] * SCALE
        qbd = jnp.concatenate([jnp.where(lane_head == h, q, 0.0) for h in range(MEM_HEADS)], axis=0).astype(BF16)
        p = _softmax_rows(_dot(qbd, mkT_ref[i].reshape(w, n_mem).astype(BF16)))
        o_all = _dot_nt(p.astype(BF16), mvT_ref[i].reshape(w, n_mem).astype(BF16))
        out = jnp.zeros((t, w), F32)
        for h in range(MEM_HEADS):
            out = out + jnp.where(lane_head == h, o_all[h * t:(h + 1) * t], 0.0)
        o_ref[i] = out


def _mem_attend_sample(qm3, mkT, mvT):
    b, t, w = qm3.shape
    sb = min(8, b)
    blk = (sb,) + mkT.shape[1:]
    return pl.pallas_call(
        functools.partial(_mem_sample_body, sb=sb),
        grid=(b // sb,),
        in_specs=[pl.BlockSpec((sb, t, w), lambda i: (i, 0, 0)),
                  pl.BlockSpec(blk, lambda i: (i, 0, 0, 0)), pl.BlockSpec(blk, lambda i: (i, 0, 0, 0))],
        out_specs=pl.BlockSpec((sb, t, w), lambda i: (i, 0, 0)),
        out_shape=jax.ShapeDtypeStruct((b, t, w), F32),
        compiler_params=_cparams(("parallel",)),
        name="mem_attend_sample",
    )(qm3, mkT, mvT)


def _mix_body(x_ref, oa_ref, u_ref, vn_ref, oc_ref, wm_ref, bz_ref, wo_ref, g_ref, b_ref, h_ref, *, tm, transposed):
    lane_grp = lax.broadcasted_iota(jnp.int32, (SGU_CHUNK, SGU_W), 1) // HEAD_DIM
    zs = []
    for c in range(tm // SGU_CHUNK):
        vn = vn_ref[c * SGU_CHUNK:(c + 1) * SGU_CHUNK, :].astype(BF16)
        z = bz_ref[...]
        for g in range(SGU_GROUPS):
            z = z + jnp.where(lane_grp == g, _dot(wm_ref[g], vn), 0.0)
        zs.append(z)
    ob = u_ref[...] * jnp.concatenate(zs, axis=0)
    oa = oa_ref[...].T if transposed else oa_ref[...]
    oc = oc_ref[...].T if transposed else oc_ref[...]
    y = _dot(oa.astype(BF16), wo_ref[0:MOBA_W, :])
    y = y + _dot(ob.astype(BF16), wo_ref[MOBA_W:MOBA_W + SGU_W, :])
    y = y + _dot(oc.astype(BF16), wo_ref[MOBA_W + SGU_W:, :])
    h_ref[...] = _layer_norm(DEEPNORM_ALPHA * x_ref[...] + y, g_ref[...], b_ref[...])


def _mix(x, oa, u, vn, oc, wmat_bf, bz, wo_bf, g, b, transposed=False):
    n, d = x.shape
    tm = min(1024, n)
    row = lambda a: pl.BlockSpec((tm, a.shape[1]), lambda i: (i, 0))
    full = lambda a: pl.BlockSpec(a.shape, lambda i: (0,) * a.ndim)
    att = (lambda a: pl.BlockSpec((a.shape[0], tm), lambda i: (0, i))) if transposed else row
    return pl.pallas_call(
        functools.partial(_mix_body, tm=tm, transposed=transposed),
        grid=(n // tm,),
        in_specs=[row(x), att(oa), row(u), row(vn), att(oc), full(wmat_bf), full(bz), full(wo_bf), full(g), full(b)],
        out_specs=row(x),
        out_shape=jax.ShapeDtypeStruct((n, d), F32),
        compiler_params=_cparams(("parallel",)),
        name="mix",
    )(x, oa, u, vn, oc, wmat_bf, bz, wo_bf, g, b)


def _rank_rows(x, n_rows, axis):
    idx = lax.broadcasted_iota(jnp.int32, x.shape, axis)
    rank = jnp.zeros(x.shape, F32)
    for j in range(n_rows):
        xj = lax.slice_in_dim(x, j, j + 1, axis=axis)
        tie = jnp.where(idx > j, 1.0, 0.0)
        rank = rank + jnp.where(xj > x, 1.0, jnp.where(xj == x, tie, 0.0))
    return rank


def _router_body(h_ref, wrT_ref, rb_ref, o_ref, *, n_exp):
    tm = h_ref.shape[0]
    gsz = n_exp // N_EXPERT_GROUPS
    scores = _sigmoid(_dot_nt(wrT_ref[...], h_ref[...], precision=HIGHEST))
    biased = scores + rb_ref[...]
    b3 = biased.reshape(N_EXPERT_GROUPS, gsz, tm)
    top2 = jnp.sum(jnp.where(_rank_rows(b3, gsz, 1) < 2.0, b3, 0.0), axis=1, keepdims=True)
    gs = jnp.broadcast_to(top2, b3.shape).reshape(n_exp, tm)
    grp = lax.broadcasted_iota(jnp.int32, (n_exp, tm), 0) // gsz
    grank = jnp.zeros((n_exp, tm), F32)
    for g2 in range(N_EXPERT_GROUPS):
        xg = gs[g2 * gsz:g2 * gsz + 1, :]
        tie = jnp.where(grp > g2, 1.0, 0.0)
        grank = grank + jnp.where(xg > gs, 1.0, jnp.where(xg == gs, tie, 0.0))
    masked = jnp.where(grank < float(TOPK_GROUPS), biased, -jnp.inf)
    w = _topk_mask(masked, TOP_K, 0) * scores
    o_ref[...] = w / jnp.sum(w, axis=0, keepdims=True) * ROUTE_SCALE


def _router(h, wrT, rb_col):
    n, d = h.shape
    n_exp = wrT.shape[0]
    tm = min(1024, n)
    return pl.pallas_call(
        functools.partial(_router_body, n_exp=n_exp),
        grid=(n // tm,),
        in_specs=[pl.BlockSpec((tm, d), lambda i: (i, 0)), pl.BlockSpec(wrT.shape, lambda i: (0, 0)),
                  pl.BlockSpec(rb_col.shape, lambda i: (0, 0))],
        out_specs=pl.BlockSpec((n_exp, tm), lambda i: (0, i)),
        out_shape=jax.ShapeDtypeStruct((n_exp, n), F32),
        compiler_params=_cparams(("parallel",)),
        name="router",
    )(h, wrT, rb_col)


def _moe_body(h_ref, gt_ref, wg_ref, wu_ref, wd_ref, sg_ref, su_ref, sd_ref, g_ref, b_ref, o_ref, acc_ref, *, eps):
    s = pl.program_id(1)
    hb = h_ref[...].astype(BF16)

    @pl.when(s == 0)
    def _():
        hs = _silu(_dot(hb, sg_ref[...])) * _dot(hb, su_ref[...])
        acc_ref[...] = _dot(hs.astype(BF16), sd_ref[...])

    lane = lax.broadcasted_iota(jnp.int32, gt_ref.shape, 1)
    hids = []
    for j in range(eps):
        gcol = jnp.sum(jnp.where(lane == s * eps + j, gt_ref[...], 0.0), axis=1, keepdims=True)
        hid = _silu(_dot(hb, wg_ref[j].astype(BF16))) * _dot(hb, wu_ref[j].astype(BF16)) * gcol
        hids.append(hid.astype(BF16))
    wd = wd_ref[...].reshape(eps * wd_ref.shape[1], wd_ref.shape[2]).astype(BF16)
    acc_ref[...] += _dot(jnp.concatenate(hids, axis=1), wd)

    @pl.when(s == pl.num_programs(1) - 1)
    def _():
        o_ref[...] = _layer_norm(DEEPNORM_ALPHA * h_ref[...] + acc_ref[...], g_ref[...], b_ref[...])


def _moe(h, gates, wg, wu, wd, sg, su, sd, g, b):
    n, d = h.shape
    n_exp, _, f = wg.shape
    tm = min(1024, n)
    eps = MOE_EXPERTS_PER_STEP
    full = lambda a: pl.BlockSpec(a.shape, lambda i, e: (0,) * a.ndim)
    return pl.pallas_call(
        functools.partial(_moe_body, eps=eps),
        grid=(n // tm, n_exp // eps),
        in_specs=[pl.BlockSpec((tm, d), lambda i, e: (i, 0)), pl.BlockSpec((tm, n_exp), lambda i, e: (i, 0)),
                  pl.BlockSpec((eps, d, f), lambda i, e: (e, 0, 0)), pl.BlockSpec((eps, d, f), lambda i, e: (e, 0, 0)),
                  pl.BlockSpec((eps, f, d), lambda i, e: (e, 0, 0)),
                  full(sg), full(su), full(sd), full(g), full(b)],
        out_specs=pl.BlockSpec((tm, d), lambda i, e: (i, 0)),
        out_shape=jax.ShapeDtypeStruct((n, d), F32),
        scratch_shapes=[pltpu.VMEM((tm, d), F32)],
        compiler_params=_cparams(("parallel", "arbitrary"), LARGE_VMEM_LIMIT),
        name="moe",
    )(h, gates, wg, wu, wd, sg, su, sd, g, b)


def _head_block_diag(q):
    lane_head = lax.broadcasted_iota(jnp.int32, q.shape, 1) // HEAD_DIM
    return jnp.concatenate([jnp.where(lane_head == h, q, 0.0) for h in range(MOBA_HEADS)], axis=0)


def _sample_keys_body(pt_ref, q_ref, *refs, ppstep):
    pages, sc_ref = refs[:ppstep], refs[ppstep]
    qbd = (_head_block_diag(q_ref[0]) * SCALE).astype(BF16)
    for i in range(ppstep):
        kt = pages[i][0].reshape(MOBA_W, PAGE_SIZE)
        sc_ref[0, :, i * PAGE_SIZE:(i + 1) * PAGE_SIZE] = _dot(qbd, kt.astype(BF16))


def _page_spec(n_pages, ppstep, i):
    return pl.BlockSpec((1, MOBA_HEADS, HEAD_DIM, PAGE_SIZE),
                        lambda b, s, pt: (pt[b * n_pages + s * ppstep + i], 0, 0, 0))


def _sample_keys(pt_flat, q3, ckT, n_pages):
    b, t, w = q3.shape
    ppstep = min(SAMPLE_PAGES_PER_STEP, n_pages)
    ht = MOBA_HEADS * t
    grid_spec = pltpu.PrefetchScalarGridSpec(
        num_scalar_prefetch=1,
        grid=(b, n_pages // ppstep),
        in_specs=[pl.BlockSpec((1, t, w), lambda b_, s, pt: (b_, 0, 0))]
        + [_page_spec(n_pages, ppstep, i) for i in range(ppstep)],
        out_specs=pl.BlockSpec((1, ht, ppstep * PAGE_SIZE), lambda b_, s, pt: (b_, 0, s)),
    )
    return pl.pallas_call(
        functools.partial(_sample_keys_body, ppstep=ppstep),
        grid_spec=grid_spec,
        out_shape=jax.ShapeDtypeStruct((b, ht, n_pages * PAGE_SIZE), F32),
        compiler_params=_cparams(("arbitrary", "arbitrary"), LARGE_VMEM_LIMIT),
        name="moba_sample_keys",
    )(pt_flat, q3, *([ckT] * ppstep))


def _sample_values_body(pt_ref, sc_ref, q_ref, kn_ref, vn_ref, *refs, ppstep, n_pages, t_new):
    pages, o_ref, (lg_s, p_s, l_s, acc_s) = refs[:ppstep], refs[ppstep], refs[ppstep + 1:]
    s = pl.program_id(1)
    ppb = MOBA_BLOCK // PAGE_SIZE
    n_fp = n_pages // ppb
    n_sel = min(MOBA_TOPK, n_fp)
    past = n_pages * PAGE_SIZE
    ht = MOBA_HEADS * t_new
    bps = ppstep // ppb

    @pl.when(s == 0)
    def _():
        qbd = _head_block_diag(q_ref[0])
        blane = lax.broadcasted_iota(jnp.int32, (ht, LANES), 1)
        gate = jnp.full((ht, LANES), -jnp.inf, F32)
        for n in range(n_fp):
            gsum = jnp.sum(sc_ref[0, :, n * MOBA_BLOCK:(n + 1) * MOBA_BLOCK], axis=1, keepdims=True)
            gate = jnp.where(blane == n, gsum, gate)
        sel = _topk_mask(gate, n_sel, 1)

        row = lax.broadcasted_iota(jnp.int32, (ht, 1), 0)
        slope = jnp.exp2(-((row // t_new) + 1).astype(F32))
        t_col = row % t_new
        qs = (qbd * SCALE).astype(BF16)
        pad = jnp.zeros((LANES - t_new, MOBA_W), F32)
        knew = jnp.concatenate([kn_ref[0], pad], axis=0).astype(BF16)
        vnew = jnp.concatenate([vn_ref[0], pad], axis=0).astype(BF16)
        l_own = _dot_nt(qs, knew) - slope * (t_col - blane).astype(F32)
        l_own = jnp.where(blane <= t_col, l_own, NEG)
        pos = lax.broadcasted_iota(jnp.int32, (ht, MOBA_BLOCK), 1)
        base = slope * (t_col - pos).astype(F32)
        m_el = jnp.full((ht, LANES), NEG, F32)
        for n in range(n_fp):
            cols = slice(n * MOBA_BLOCK, (n + 1) * MOBA_BLOCK)
            ln = sc_ref[0, :, cols] - base - slope * float(past - n * MOBA_BLOCK)
            ln = jnp.where(sel[:, n:n + 1] > 0.5, ln, NEG)
            m_el = jnp.maximum(m_el, jnp.maximum(ln[:, :LANES], ln[:, LANES:]))
            lg_s[:, cols] = ln
        m = jnp.maximum(jnp.max(l_own, axis=1, keepdims=True), jnp.max(m_el, axis=1, keepdims=True))
        p_own = jnp.exp(l_own - m)
        s_el = jnp.zeros((ht, LANES), F32)
        for n in range(n_fp):
            pn = jnp.exp(lg_s[:, n * MOBA_BLOCK:(n + 1) * MOBA_BLOCK] - m)
            s_el = s_el + (pn[:, :LANES] + pn[:, LANES:])
            p_s[n // bps, :, (n % bps) * MOBA_BLOCK:(n % bps + 1) * MOBA_BLOCK] = pn.astype(BF16)
        lsum = jnp.sum(p_own, axis=1, keepdims=True) + jnp.sum(s_el, axis=1, keepdims=True)
        l_s[...] = jnp.broadcast_to(lsum, (ht, LANES))
        acc_s[...] = _dot(p_own.astype(BF16), vnew)

    vt = jnp.concatenate([pages[i][0].reshape(MOBA_W, PAGE_SIZE).astype(BF16) for i in range(ppstep)], axis=1)
    acc_s[...] += _dot_nt(p_s[s], vt)

    @pl.when(s == pl.num_programs(1) - 1)
    def _():
        o_all = acc_s[...] / l_s[:, 0:1]
        lane_head = lax.broadcasted_iota(jnp.int32, (t_new, MOBA_W), 1) // HEAD_DIM
        out = jnp.zeros((t_new, MOBA_W), F32)
        for h in range(MOBA_HEADS):
            out = out + jnp.where(lane_head == h, o_all[h * t_new:(h + 1) * t_new], 0.0)
        o_ref[0] = out


def _sample_values(pt_flat, scores, q3, kn3, vn3, cvT, n_pages):
    b, t, w = q3.shape
    ppstep = min(SAMPLE_PAGES_PER_STEP, n_pages)
    ht = MOBA_HEADS * t
    n_steps = n_pages // ppstep
    small = lambda a: pl.BlockSpec((1,) + a.shape[1:], lambda b_, s, pt: (b_,) + (0,) * (a.ndim - 1))
    grid_spec = pltpu.PrefetchScalarGridSpec(
        num_scalar_prefetch=1,
        grid=(b, n_steps),
        in_specs=[small(scores), small(q3), small(kn3), small(vn3)]
        + [_page_spec(n_pages, ppstep, i) for i in range(ppstep)],
        out_specs=pl.BlockSpec((1, t, w), lambda b_, s, pt: (b_, 0, 0)),
        scratch_shapes=[pltpu.VMEM((ht, n_pages * PAGE_SIZE), F32),
                        pltpu.VMEM((n_steps, ht, ppstep * PAGE_SIZE), BF16),
                        pltpu.VMEM((ht, LANES), F32),
                        pltpu.VMEM((ht, w), F32)],
    )
    return pl.pallas_call(
        functools.partial(_sample_values_body, ppstep=ppstep, n_pages=n_pages, t_new=t),
        grid_spec=grid_spec,
        out_shape=jax.ShapeDtypeStruct((b, t, w), F32),
        compiler_params=_cparams(("arbitrary", "arbitrary"), LARGE_VMEM_LIMIT),
        name="moba_sample_values",
    )(pt_flat, scores, q3, kn3, vn3, *([cvT] * ppstep))


def _sgu_block_diag(w_s, t_new):
    g = w_s.shape[0]
    reps = SGU_CHUNK // t_new
    small = jnp.tril(w_s)[:, :t_new, :t_new]
    eye = jnp.eye(reps, dtype=w_s.dtype)
    return jnp.einsum("rq,gts->grtqs", eye, small).reshape(g, SGU_CHUNK, SGU_CHUNK)


def _sgu_bias(b_s, rows):
    return jnp.repeat(b_s[:, :rows].T, HEAD_DIM, axis=1)


def _ffn(h1, wrT, rb_col, moe_w, g, b):
    gates = _router(h1, wrT, rb_col).T
    return _moe(h1, gates, *moe_w, g, b)


def kernel(x_prompt, x_sample, mem_prompt, cache_k, cache_v, page_table, cache_mem_k, cache_mem_v, w_in, w_out, w_mem_kv, sgu_ln_g, sgu_ln_b, sgu_w_s, sgu_b_s, ln1_g, ln1_b, w_router, router_bias, w_gate, w_up, w_down, ws_gate, ws_up, ws_down, ln2_g, ln2_b):
    assert w_in.shape[0] == DEPTH and x_prompt.shape[0] == 1
    _, seq, d_model = x_prompt.shape
    db, t_new, _ = x_sample.shape
    n_pages = page_table.shape[1]
    assert n_pages % (MOBA_BLOCK // PAGE_SIZE) == 0 and SGU_CHUNK % t_new == 0
    n_s = db * t_new
    assert n_s % SGU_CHUNK == 0 and seq % (8 * MOBA_BLOCK) == 0 and seq % (ATT_GROUP * MOBA_BLOCK) == 0
    assert n_pages % min(SAMPLE_PAGES_PER_STEP, n_pages) == 0

    w_in_bf = w_in[0].astype(BF16)
    wo_bf = w_out[0].astype(BF16)
    wmem_bf = w_mem_kv[0].astype(BF16)
    sg_g, sg_b = sgu_ln_g, sgu_ln_b
    moe_w = (w_gate[0], w_up[0], w_down[0],
             ws_gate[0].astype(BF16), ws_up[0].astype(BF16), ws_down[0].astype(BF16))
    wrT = jnp.swapaxes(w_router[0], 0, 1)
    rb_col = router_bias[0][:, None]

    xp = x_prompt[0]
    qT, k, k_bf, kTp, vTp, vTg, u, svn, qmT = _in_proj_prompt(xp, w_in_bf, sg_g, sg_b)
    mkv, mkvT = _mem_kv(mem_prompt[0], wmem_bf)
    oT = _moba_prompt(qT, k_bf, vTg, _k_block_mean(k))
    ocT = _mem_attend_prompt(qmT, mkv, mkvT)
    wm_p = jnp.tril(sgu_w_s[0]).astype(BF16)
    h1 = _mix(xp, oT, u, svn, ocT, wm_p, _sgu_bias(sgu_b_s[0], SGU_CHUNK), wo_bf, ln1_g, ln1_b, transposed=True)
    y_prompt = _ffn(h1, wrT, rb_col, moe_w, ln2_g, ln2_b)[None]

    n_pg = seq // PAGE_SIZE
    to_pages = lambda zT: jnp.swapaxes(zT.reshape(1, 1, n_pg, MOBA_HEADS, HEAD_DIM, PAGE_SIZE), -1, -2)
    k_prompt, v_prompt = to_pages(kTp), to_pages(vTp)
    n_mem = mkv.shape[0]
    mem_k_prompt = mkv[:, :MEM_W].reshape(1, 1, n_mem, MEM_HEADS, HEAD_DIM)
    mem_v_prompt = mkv[:, MEM_W:].reshape(1, 1, n_mem, MEM_HEADS, HEAD_DIM)

    xs = x_sample.reshape(n_s, d_model)
    q, k, v, u, svn, qm = _in_proj(xs, w_in_bf, sg_g, sg_b)
    heads = lambda z: z.reshape(db, t_new, MOBA_HEADS, HEAD_DIM).transpose(0, 2, 1, 3)
    k4, v4 = heads(k), heads(v)
    ckT = jnp.swapaxes(cache_k[0], -1, -2)
    cvT = jnp.swapaxes(cache_v[0], -1, -2)
    pt_flat = page_table.reshape(-1)
    three = lambda z: z.reshape(db, t_new, MOBA_W)
    scores = _sample_keys(pt_flat, three(q), ckT, n_pages)
    o_a = _sample_values(pt_flat, scores, three(q), three(k), three(v), cvT, n_pages).reshape(n_s, MOBA_W)
    mkT = cache_mem_k[0].transpose(0, 2, 3, 1)
    mvT = cache_mem_v[0].transpose(0, 2, 3, 1)
    o_c = _mem_attend_sample(qm.reshape(db, t_new, MEM_W), mkT, mvT).reshape(n_s, MEM_W)
    wm_s = _sgu_block_diag(sgu_w_s[0], t_new).astype(BF16)
    bz_s = jnp.tile(_sgu_bias(sgu_b_s[0], t_new), (SGU_CHUNK // t_new, 1))
    h1 = _mix(xs, o_a, u, svn, o_c, wm_s, bz_s, wo_bf, ln1_g, ln1_b)
    y_sample = _ffn(h1, wrT, rb_col, moe_w, ln2_g, ln2_b).reshape(db, t_new, d_model)

    k_sample, v_sample = k4[None], v4[None]
    sgu_v_sample = svn.reshape(1, db, t_new, SGU_W)
    return (y_prompt, y_sample, k_prompt, v_prompt, k_sample, v_sample, mem_k_prompt, mem_v_prompt, sgu_v_sample)
```

```python
import functools

import jax
import jax.numpy as jnp
from jax import lax
from jax.experimental import pallas as pl
from jax.experimental.pallas import tpu as pltpu

F32 = jnp.float32
BF16 = jnp.bfloat16

HEAD_DIM = 64
MOBA_HEADS = 8
SGU_GROUPS = 4
MEM_HEADS = 4
MOBA_W = MOBA_HEADS * HEAD_DIM
SGU_W = SGU_GROUPS * HEAD_DIM
MEM_W = MEM_HEADS * HEAD_DIM
MOBA_BLOCK = 256
MOBA_TOPK = 3
Q_BLOCK = 128
SGU_CHUNK = 128
PAGE_SIZE = 128
N_EXPERT_GROUPS = 8
TOPK_GROUPS = 4
TOP_K = 8
ROUTE_SCALE = 2.5
LN_EPS = 1e-5
NEG = -1e30
DEPTH = 1
DEEPNORM_ALPHA = (2 * DEPTH) ** 0.25
SCALE = HEAD_DIM ** -0.5
LOG2E = 1.4426950408889634
ATT_GROUP = 8
SAMPLE_PAGES_PER_STEP = 64
MOE_EXPERTS_PER_STEP = 4
LANES = 128
VMEM_LIMIT = 48 * 1024 * 1024
LARGE_VMEM_LIMIT = 58 * 1024 * 1024
HIGHEST = lax.Precision.HIGHEST


def _cparams(sem, limit=VMEM_LIMIT):
    return pltpu.CompilerParams(dimension_semantics=sem, vmem_limit_bytes=limit)


def _gelu(x):
    return x * (0.5 * (1.0 + jnp.tanh(0.7978845608028654 * (x + 0.044715 * (x * x * x)))))


def _sigmoid(x):
    return 1.0 / (1.0 + jnp.exp(-x))


def _silu(x):
    return x * _sigmoid(x)


def _layer_norm(x, g, b):
    mu = jnp.mean(x, axis=-1, keepdims=True)
    xc = x - mu
    var = jnp.mean(xc * xc, axis=-1, keepdims=True)
    return xc * lax.rsqrt(var + LN_EPS) * g + b


def _topk_mask(x, k, axis):
    n = x.shape[axis]
    idx = lax.broadcasted_iota(jnp.int32, x.shape, axis).astype(F32)
    sel = jnp.zeros(x.shape, F32)
    for _ in range(k):
        mx = jnp.max(x, axis=axis, keepdims=True)
        first = jnp.min(jnp.where(x == mx, idx, float(n)), axis=axis, keepdims=True)
        hit = idx == first
        sel = jnp.where(hit, 1.0, sel)
        x = jnp.where(hit, -jnp.inf, x)
    return sel


def _dot(a, b):
    return jnp.dot(a, b, preferred_element_type=F32)


def _dot_nt(a, b, precision=None):
    return lax.dot_general(a, b, (((1,), (1,)), ((), ())), precision=precision, preferred_element_type=F32)


def _project(x_ref, w_ref, g_ref, b_ref):
    x = x_ref[...].astype(BF16)
    c0, c1, c2, c3, c4 = MOBA_W, 2 * MOBA_W, 3 * MOBA_W, 3 * MOBA_W + SGU_W, 3 * MOBA_W + 2 * SGU_W
    q = _dot(x, w_ref[:, 0:c0])
    k = _dot(x, w_ref[:, c0:c1])
    v = _dot(x, w_ref[:, c1:c2])
    u = _gelu(_dot(x, w_ref[:, c2:c3]))
    sv = _layer_norm(_gelu(_dot(x, w_ref[:, c3:c4])), g_ref[...], b_ref[...])
    qm = _dot(x, w_ref[:, c4:c4 + MEM_W])
    return q, k, v, u, sv, qm


def _inproj_body(x_ref, w_ref, g_ref, b_ref, *out_refs):
    for ref, val in zip(out_refs, _project(x_ref, w_ref, g_ref, b_ref), strict=True):
        ref[...] = val


def _inproj_prompt_body(x_ref, w_ref, g_ref, b_ref, qT_ref, k_ref, kbf_ref, kTp_ref, vTp_ref, vTg_ref,
                        u_ref, sv_ref, qm_ref):
    q, k, v, u, sv, qm = _project(x_ref, w_ref, g_ref, b_ref)
    tm = q.shape[0]
    qT_ref[...] = q.T
    k_ref[...] = k
    kbf_ref[...] = k.astype(BF16)
    kT, vT = k.T, v.T
    for p in range(tm // PAGE_SIZE):
        kTp_ref[p] = kT[:, p * PAGE_SIZE:(p + 1) * PAGE_SIZE]
        vTp_ref[p] = vT[:, p * PAGE_SIZE:(p + 1) * PAGE_SIZE]
    vT_bf = vT.astype(BF16)
    ones = jnp.ones((ATT_ONES, tm), BF16)
    for c in range(N_PAIRS):
        vTg_ref[0, c * ATT_SLAB:c * ATT_SLAB + ATT_PAIR_W, :] = vT_bf[c * ATT_PAIR_W:(c + 1) * ATT_PAIR_W]
        vTg_ref[0, c * ATT_SLAB + ATT_PAIR_W:(c + 1) * ATT_SLAB, :] = ones
    u_ref[...] = u
    sv_ref[...] = sv
    qm_ref[...] = qm.T


def _in_proj_prompt(x, w_in_bf, sgu_g, sgu_b):
    n, d = x.shape
    tm = 512
    gk = ATT_GROUP * MOBA_BLOCK
    per_grp = gk // tm
    ppt = tm // PAGE_SIZE
    row = lambda w: pl.BlockSpec((tm, w), lambda i: (i, 0))
    full = lambda a: pl.BlockSpec(a.shape, lambda i: (0,) * a.ndim)
    pages = pl.BlockSpec((ppt, MOBA_W, PAGE_SIZE), lambda i: (i, 0, 0))
    sds = jax.ShapeDtypeStruct
    return pl.pallas_call(
        _inproj_prompt_body,
        grid=(n // tm,),
        in_specs=[row(d), full(w_in_bf), full(sgu_g), full(sgu_b)],
        out_specs=[pl.BlockSpec((MOBA_W, tm), lambda i: (0, i)), row(MOBA_W), row(MOBA_W), pages, pages,
                   pl.BlockSpec((1, N_PAIRS * ATT_SLAB, tm), lambda i: (i // per_grp, 0, i % per_grp)),
                   row(SGU_W), row(SGU_W), pl.BlockSpec((MEM_W, tm), lambda i: (0, i))],
        out_shape=[sds((MOBA_W, n), F32), sds((n, MOBA_W), F32), sds((n, MOBA_W), BF16),
                   sds((n // PAGE_SIZE, MOBA_W, PAGE_SIZE), F32), sds((n // PAGE_SIZE, MOBA_W, PAGE_SIZE), F32),
                   sds((n // gk, N_PAIRS * ATT_SLAB, gk), BF16),
                   sds((n, SGU_W), F32), sds((n, SGU_W), F32), sds((MEM_W, n), F32)],
        compiler_params=_cparams(("parallel",)),
        name="in_proj_prompt",
    )(x, w_in_bf, sgu_g, sgu_b)


def _in_proj(x, w_in_bf, sgu_g, sgu_b):
    n, d = x.shape
    tm = min(512, n)
    in_w = w_in_bf.shape[1]
    row = lambda w: pl.BlockSpec((tm, w), lambda i: (i, 0))
    full = lambda a: pl.BlockSpec(a.shape, lambda i: (0,) * a.ndim)
    widths = (MOBA_W, MOBA_W, MOBA_W, SGU_W, SGU_W, MEM_W)
    return pl.pallas_call(
        _inproj_body,
        grid=(n // tm,),
        in_specs=[row(d), full(w_in_bf), full(sgu_g), full(sgu_b)],
        out_specs=[row(w) for w in widths],
        out_shape=[jax.ShapeDtypeStruct((n, w), F32) for w in widths],
        compiler_params=_cparams(("parallel",)),
        name="in_proj",
    )(x, w_in_bf, sgu_g, sgu_b)


def _memkv_body(m_ref, w_ref, o_ref, oT_ref):
    kv = _dot(m_ref[...].astype(BF16), w_ref[...])
    o_ref[...] = kv
    oT_ref[...] = kv.T


def _mem_kv(mem, w_bf):
    n, _ = mem.shape
    w = w_bf.shape[1]
    return pl.pallas_call(
        _memkv_body,
        out_shape=[jax.ShapeDtypeStruct((n, w), F32), jax.ShapeDtypeStruct((w, n), F32)],
        compiler_params=pltpu.CompilerParams(vmem_limit_bytes=VMEM_LIMIT),
        name="mem_kv",
    )(mem, w_bf)


def _kmean_body(k_ref, o_ref):
    kb = k_ref[...].reshape(8, MOBA_BLOCK, k_ref.shape[1])
    o_ref[...] = jnp.sum(kb, axis=1) * (1.0 / MOBA_BLOCK)


def _k_block_mean(k):
    s, w = k.shape
    nblk = s // MOBA_BLOCK
    return pl.pallas_call(
        _kmean_body,
        grid=(nblk // 8,),
        in_specs=[pl.BlockSpec((8 * MOBA_BLOCK, w), lambda i: (i, 0))],
        out_specs=pl.BlockSpec((8, w), lambda i: (i, 0)),
        out_shape=jax.ShapeDtypeStruct((nblk, w), F32),
        compiler_params=_cparams(("parallel",)),
        name="k_block_mean",
    )(k)


ATT_PAIR_W = 2 * HEAD_DIM
ATT_ONES = 16
ATT_SLAB = ATT_PAIR_W + ATT_ONES
N_PAIRS = MOBA_HEADS // 2


def _moba_prompt_body(qT_ref, k_ref, vTg_ref, km_ref, oT_ref, bias_s, dmat_s, qs_s, *, nblk):
    qi = pl.program_id(0)
    own = qi // (MOBA_BLOCK // Q_BLOCK)
    half = qi % (MOBA_BLOCK // Q_BLOCK)
    w2 = 2 * Q_BLOCK
    gk = ATT_GROUP * MOBA_BLOCK

    lane = lax.broadcasted_iota(jnp.int32, (1, w2), 1)
    tl = lane % Q_BLOCK
    sl = lax.broadcasted_iota(jnp.int32, (MOBA_BLOCK, w2), 0)
    n_iota = lax.broadcasted_iota(jnp.int32, (nblk, w2), 0)
    blk_off = (n_iota * MOBA_BLOCK - qi * Q_BLOCK).astype(F32)
    causal = half * Q_BLOCK + tl - sl >= 0
    z = jnp.zeros((HEAD_DIM, Q_BLOCK), F32)

    state = []
    for c in range(N_PAIRS):
        head = 2 * c + (lane >= Q_BLOCK).astype(jnp.int32)
        slope = jnp.exp2(-(head + 1).astype(F32)) * LOG2E

        @pl.when(qi == 0)
        def _():
            dmat_s[c] = slope * sl.astype(F32)

        qT = qT_ref[c * ATT_PAIR_W:(c + 1) * ATT_PAIR_W, :]
        qbd = jnp.concatenate(
            [jnp.concatenate([qT[:HEAD_DIM], z], axis=1), jnp.concatenate([z, qT[HEAD_DIM:]], axis=1)], axis=0)
        qs_s[c] = (qbd * (SCALE * LOG2E)).astype(BF16)

        gate = jnp.dot(km_ref[:, c * ATT_PAIR_W:(c + 1) * ATT_PAIR_W], qbd, precision=HIGHEST,
                       preferred_element_type=F32)
        gate = jnp.where(n_iota < own, gate, -jnp.inf)
        sel = jnp.logical_and(_topk_mask(gate, MOBA_TOPK, 0) > 0.5, n_iota < own)
        bias_s[c] = jnp.where(sel, slope * blk_off, NEG)

        k_own = k_ref[pl.ds(pl.multiple_of(own * MOBA_BLOCK, MOBA_BLOCK), MOBA_BLOCK),
                      c * ATT_PAIR_W:(c + 1) * ATT_PAIR_W]
        s0 = _dot(k_own, qs_s[c]) + dmat_s[c] - slope * (half * Q_BLOCK).astype(F32)
        s0 = jnp.where(causal, s0, NEG)
        m0 = jnp.max(s0, axis=0, keepdims=True)
        p0 = jnp.exp2(s0 - m0).astype(BF16)
        col0 = pl.multiple_of((own % ATT_GROUP) * MOBA_BLOCK, MOBA_BLOCK)
        res0 = _dot(vTg_ref[own // ATT_GROUP, c * ATT_SLAB:(c + 1) * ATT_SLAB, pl.ds(col0, MOBA_BLOCK)], p0)
        state.append((m0, res0[ATT_PAIR_W:ATT_PAIR_W + 1], res0[:ATT_PAIR_W]))

    def scores(j, c):
        out = []
        for g in range(ATT_GROUP):
            row0 = pl.multiple_of((j * ATT_GROUP + g) * MOBA_BLOCK, MOBA_BLOCK)
            out.append(_dot(k_ref[pl.ds(row0, MOBA_BLOCK), c * ATT_PAIR_W:(c + 1) * ATT_PAIR_W], qs_s[c]) + dmat_s[c])
        return out

    def body(j, carry):
        out = []
        ahead = [scores(j, 0), scores(j, 1)]
        for c in range(N_PAIRS):
            m, l, acc = carry[c]
            parts = ahead.pop(0)
            if c + 2 < N_PAIRS:
                ahead.append(scores(j, c + 2))
            biases = []
            m_new = m
            for g in range(ATT_GROUP):
                bg = bias_s[c, pl.ds(j * ATT_GROUP + g, 1), :]
                m_new = jnp.maximum(m_new, jnp.max(parts[g], axis=0, keepdims=True) + bg)
                biases.append(bg)
            a = jnp.exp2(m - m_new)
            pj = jnp.concatenate(
                [jnp.exp2(parts[g] - (m_new - biases[g])).astype(BF16) for g in range(ATT_GROUP)], axis=0)
            res = _dot(vTg_ref[j, c * ATT_SLAB:(c + 1) * ATT_SLAB, :], pj)
            out.append((m_new, a * l + res[ATT_PAIR_W:ATT_PAIR_W + 1], a * acc + res[:ATT_PAIR_W]))
        return tuple(out)

    n_trip = (own + (ATT_GROUP - 1)) // ATT_GROUP
    n_pair = n_trip // 2
    state = lax.fori_loop(0, n_pair, lambda i, st: body(2 * i + 1, body(2 * i, st)), tuple(state))
    state = lax.fori_loop(2 * n_pair, n_trip, body, state)
    for c in range(N_PAIRS):
        _, l, acc = state[c]
        inv = 1.0 / l
        r0 = c * ATT_PAIR_W
        oT_ref[r0:r0 + HEAD_DIM, :] = acc[0:HEAD_DIM, 0:Q_BLOCK] * inv[:, 0:Q_BLOCK]
        oT_ref[r0 + HEAD_DIM:r0 + ATT_PAIR_W, :] = acc[HEAD_DIM:, Q_BLOCK:] * inv[:, Q_BLOCK:]


def _moba_prompt(qT, k_bf, vTg, kmean):
    w, s = qT.shape
    nblk = kmean.shape[0]
    resident = lambda a: pl.BlockSpec(a.shape, lambda i: (0,) * a.ndim, pipeline_mode=pl.Buffered(1))
    return pl.pallas_call(
        functools.partial(_moba_prompt_body, nblk=nblk),
        grid=(s // Q_BLOCK,),
        in_specs=[pl.BlockSpec((w, Q_BLOCK), lambda i: (0, i)), resident(k_bf), resident(vTg), resident(kmean)],
        out_specs=pl.BlockSpec((w, Q_BLOCK), lambda i: (0, i)),
        out_shape=jax.ShapeDtypeStruct((w, s), F32),
        scratch_shapes=[pltpu.VMEM((N_PAIRS, nblk, 2 * Q_BLOCK), F32),
                        pltpu.VMEM((N_PAIRS, MOBA_BLOCK, 2 * Q_BLOCK), F32),
                        pltpu.VMEM((N_PAIRS, ATT_PAIR_W, 2 * Q_BLOCK), BF16)],
        compiler_params=_cparams(("arbitrary",), LARGE_VMEM_LIMIT),
        name="moba_prompt",
    )(qT, k_bf, vTg, kmean)


def _softmax_rows(logits):
    m = jnp.max(logits, axis=-1, keepdims=True)
    e = jnp.exp(logits - m)
    return e / jnp.sum(e, axis=-1, keepdims=True)


def _mem_prompt_body(qmT_ref, mkv_ref, mkvT_ref, oT_ref):
    for h in range(MEM_HEADS):
        lo, hi = h * HEAD_DIM, (h + 1) * HEAD_DIM
        qT = (qmT_ref[lo:hi, :] * SCALE).astype(BF16)
        lg = _dot(mkv_ref[:, lo:hi].astype(BF16), qT)
        e = jnp.exp(lg - jnp.max(lg, axis=0, keepdims=True))
        den = jnp.sum(e, axis=0, keepdims=True)
        oT_ref[lo:hi, :] = _dot(mkvT_ref[MEM_W + lo:MEM_W + hi, :].astype(BF16), e.astype(BF16)) / den


def _mem_attend_prompt(qmT, mkv, mkvT):
    w, n = qmT.shape
    tm = min(512, n)
    return pl.pallas_call(
        _mem_prompt_body,
        grid=(n // tm,),
        in_specs=[pl.BlockSpec((w, tm), lambda i: (0, i)), pl.BlockSpec(mkv.shape, lambda i: (0, 0)),
                  pl.BlockSpec(mkvT.shape, lambda i: (0, 0))],
        out_specs=pl.BlockSpec((w, tm), lambda i: (0, i)),
        out_shape=jax.ShapeDtypeStruct((w, n), F32),
        compiler_params=_cparams(("parallel",)),
        name="mem_attend_prompt",
    )(qmT, mkv, mkvT)


def _mem_sample_body(qm_ref, mkT_ref, mvT_ref, o_ref, *, sb):
    t, w = qm_ref.shape[1:]
    n_mem = mkT_ref.shape[-1]
    lane_head = lax.broadcasted_iota(jnp.int32, (t, w), 1) // HEAD_DIM
    for i in range(sb):
        q = qm_ref[i] * SCALE
        qbd = jnp.concatenate([jnp.where(lane_head == h, q, 0.0) for h in range(MEM_HEADS)], axis=0).astype(BF16)
        p = _softmax_rows(_dot(qbd, mkT_ref[i].reshape(w, n_mem).astype(BF16)))
        o_all = _dot_nt(p.astype(BF16), mvT_ref[i].reshape(w, n_mem).astype(BF16))
        out = jnp.zeros((t, w), F32)
        for h in range(MEM_HEADS):
            out = out + jnp.where(lane_head == h, o_all[h * t:(h + 1) * t], 0.0)
        o_ref[i] = out


def _mem_attend_sample(qm3, mkT, mvT):
    b, t, w = qm3.shape
    sb = min(8, b)
    blk = (sb,) + mkT.shape[1:]
    return pl.pallas_call(
        functools.partial(_mem_sample_body, sb=sb),
        grid=(b // sb,),
        in_specs=[pl.BlockSpec((sb, t, w), lambda i: (i, 0, 0)),
                  pl.BlockSpec(blk, lambda i: (i, 0, 0, 0)), pl.BlockSpec(blk, lambda i: (i, 0, 0, 0))],
        out_specs=pl.BlockSpec((sb, t, w), lambda i: (i, 0, 0)),
        out_shape=jax.ShapeDtypeStruct((b, t, w), F32),
        compiler_params=_cparams(("parallel",)),
        name="mem_attend_sample",
    )(qm3, mkT, mvT)


def _mix_body(x_ref, oa_ref, u_ref, vn_ref, oc_ref, wm_ref, bz_ref, wo_ref, g_ref, b_ref, h_ref, *, tm, transposed):
    lane_grp = lax.broadcasted_iota(jnp.int32, (SGU_CHUNK, SGU_W), 1) // HEAD_DIM
    zs = []
    for c in range(tm // SGU_CHUNK):
        vn = vn_ref[c * SGU_CHUNK:(c + 1) * SGU_CHUNK, :].astype(BF16)
        z = bz_ref[...]
        for g in range(SGU_GROUPS):
            z = z + jnp.where(lane_grp == g, _dot(wm_ref[g], vn), 0.0)
        zs.append(z)
    ob = u_ref[...] * jnp.concatenate(zs, axis=0)
    oa = oa_ref[...].T if transposed else oa_ref[...]
    oc = oc_ref[...].T if transposed else oc_ref[...]
    y = _dot(oa.astype(BF16), wo_ref[0:MOBA_W, :])
    y = y + _dot(ob.astype(BF16), wo_ref[MOBA_W:MOBA_W + SGU_W, :])
    y = y + _dot(oc.astype(BF16), wo_ref[MOBA_W + SGU_W:, :])
    h_ref[...] = _layer_norm(DEEPNORM_ALPHA * x_ref[...] + y, g_ref[...], b_ref[...])


def _mix(x, oa, u, vn, oc, wmat_bf, bz, wo_bf, g, b, transposed=False):
    n, d = x.shape
    tm = min(1024, n)
    row = lambda a: pl.BlockSpec((tm, a.shape[1]), lambda i: (i, 0))
    full = lambda a: pl.BlockSpec(a.shape, lambda i: (0,) * a.ndim)
    att = (lambda a: pl.BlockSpec((a.shape[0], tm), lambda i: (0, i))) if transposed else row
    return pl.pallas_call(
        functools.partial(_mix_body, tm=tm, transposed=transposed),
        grid=(n // tm,),
        in_specs=[row(x), att(oa), row(u), row(vn), att(oc), full(wmat_bf), full(bz), full(wo_bf), full(g), full(b)],
        out_specs=row(x),
        out_shape=jax.ShapeDtypeStruct((n, d), F32),
        compiler_params=_cparams(("parallel",)),
        name="mix",
    )(x, oa, u, vn, oc, wmat_bf, bz, wo_bf, g, b)


def _rank_rows(x, n_rows, axis):
    idx = lax.broadcasted_iota(jnp.int32, x.shape, axis)
    rank = jnp.zeros(x.shape, F32)
    for j in range(n_rows):
        xj = lax.slice_in_dim(x, j, j + 1, axis=axis)
        tie = jnp.where(idx > j, 1.0, 0.0)
        rank = rank + jnp.where(xj > x, 1.0, jnp.where(xj == x, tie, 0.0))
    return rank


def _router_body(h_ref, wrT_ref, rb_ref, o_ref, *, n_exp):
    tm = h_ref.shape[0]
    gsz = n_exp // N_EXPERT_GROUPS
    scores = _sigmoid(_dot_nt(wrT_ref[...], h_ref[...], precision=HIGHEST))
    biased = scores + rb_ref[...]
    b3 = biased.reshape(N_EXPERT_GROUPS, gsz, tm)
    top2 = jnp.sum(jnp.where(_rank_rows(b3, gsz, 1) < 2.0, b3, 0.0), axis=1, keepdims=True)
    gs = jnp.broadcast_to(top2, b3.shape).reshape(n_exp, tm)
    grp = lax.broadcasted_iota(jnp.int32, (n_exp, tm), 0) // gsz
    grank = jnp.zeros((n_exp, tm), F32)
    for g2 in range(N_EXPERT_GROUPS):
        xg = gs[g2 * gsz:g2 * gsz + 1, :]
        tie = jnp.where(grp > g2, 1.0, 0.0)
        grank = grank + jnp.where(xg > gs, 1.0, jnp.where(xg == gs, tie, 0.0))
    masked = jnp.where(grank < float(TOPK_GROUPS), biased, -jnp.inf)
    w = _topk_mask(masked, TOP_K, 0) * scores
    o_ref[...] = w / jnp.sum(w, axis=0, keepdims=True) * ROUTE_SCALE


def _router(h, wrT, rb_col):
    n, d = h.shape
    n_exp = wrT.shape[0]
    tm = min(1024, n)
    return pl.pallas_call(
        functools.partial(_router_body, n_exp=n_exp),
        grid=(n // tm,),
        in_specs=[pl.BlockSpec((tm, d), lambda i: (i, 0)), pl.BlockSpec(wrT.shape, lambda i: (0, 0)),
                  pl.BlockSpec(rb_col.shape, lambda i: (0, 0))],
        out_specs=pl.BlockSpec((n_exp, tm), lambda i: (0, i)),
        out_shape=jax.ShapeDtypeStruct((n_exp, n), F32),
        compiler_params=_cparams(("parallel",)),
        name="router",
    )(h, wrT, rb_col)


def _moe_body(h_ref, gt_ref, wg_ref, wu_ref, wd_ref, sg_ref, su_ref, sd_ref, g_ref, b_ref, o_ref, acc_ref, *, eps):
    s = pl.program_id(1)
    hb = h_ref[...].astype(BF16)

    @pl.when(s == 0)
    def _():
        hs = _silu(_dot(hb, sg_ref[...])) * _dot(hb, su_ref[...])
        acc_ref[...] = _dot(hs.astype(BF16), sd_ref[...])

    lane = lax.broadcasted_iota(jnp.int32, gt_ref.shape, 1)
    hids = []
    for j in range(eps):
        gcol = jnp.sum(jnp.where(lane == s * eps + j, gt_ref[...], 0.0), axis=1, keepdims=True)
        hid = _silu(_dot(hb, wg_ref[j].astype(BF16))) * _dot(hb, wu_ref[j].astype(BF16)) * gcol
        hids.append(hid.astype(BF16))
    wd = wd_ref[...].reshape(eps * wd_ref.shape[1], wd_ref.shape[2]).astype(BF16)
    acc_ref[...] += _dot(jnp.concatenate(hids, axis=1), wd)

    @pl.when(s == pl.num_programs(1) - 1)
    def _():
        o_ref[...] = _layer_norm(DEEPNORM_ALPHA * h_ref[...] + acc_ref[...], g_ref[...], b_ref[...])


def _moe(h, gates, wg, wu, wd, sg, su, sd, g, b):
    n, d = h.shape
    n_exp, _, f = wg.shape
    tm = min(1024, n)
    eps = MOE_EXPERTS_PER_STEP
    full = lambda a: pl.BlockSpec(a.shape, lambda i, e: (0,) * a.ndim)
    return pl.pallas_call(
        functools.partial(_moe_body, eps=eps),
        grid=(n // tm, n_exp // eps),
        in_specs=[pl.BlockSpec((tm, d), lambda i, e: (i, 0)), pl.BlockSpec((tm, n_exp), lambda i, e: (i, 0)),
                  pl.BlockSpec((eps, d, f), lambda i, e: (e, 0, 0)), pl.BlockSpec((eps, d, f), lambda i, e: (e, 0, 0)),
                  pl.BlockSpec((eps, f, d), lambda i, e: (e, 0, 0)),
                  full(sg), full(su), full(sd), full(g), full(b)],
        out_specs=pl.BlockSpec((tm, d), lambda i, e: (i, 0)),
        out_shape=jax.ShapeDtypeStruct((n, d), F32),
        scratch_shapes=[pltpu.VMEM((tm, d), F32)],
        compiler_params=_cparams(("parallel", "arbitrary"), LARGE_VMEM_LIMIT),
        name="moe",
    )(h, gates, wg, wu, wd, sg, su, sd, g, b)


def _head_block_diag(q):
    lane_head = lax.broadcasted_iota(jnp.int32, q.shape, 1) // HEAD_DIM
    return jnp.concatenate([jnp.where(lane_head == h, q, 0.0) for h in range(MOBA_HEADS)], axis=0)


def _sample_keys_body(pt_ref, q_ref, *refs, ppstep):
    pages, sc_ref = refs[:ppstep], refs[ppstep]
    qbd = (_head_block_diag(q_ref[0]) * SCALE).astype(BF16)
    for i in range(ppstep):
        kt = pages[i][0].reshape(MOBA_W, PAGE_SIZE)
        sc_ref[0, :, i * PAGE_SIZE:(i + 1) * PAGE_SIZE] = _dot(qbd, kt.astype(BF16))


def _page_spec(n_pages, ppstep, i):
    return pl.BlockSpec((1, MOBA_HEADS, HEAD_DIM, PAGE_SIZE),
                        lambda b, s, pt: (pt[b * n_pages + s * ppstep + i], 0, 0, 0))


def _sample_keys(pt_flat, q3, ckT, n_pages):
    b, t, w = q3.shape
    ppstep = min(SAMPLE_PAGES_PER_STEP, n_pages)
    ht = MOBA_HEADS * t
    grid_spec = pltpu.PrefetchScalarGridSpec(
        num_scalar_prefetch=1,
        grid=(b, n_pages // ppstep),
        in_specs=[pl.BlockSpec((1, t, w), lambda b_, s, pt: (b_, 0, 0))]
        + [_page_spec(n_pages, ppstep, i) for i in range(ppstep)],
        out_specs=pl.BlockSpec((1, ht, ppstep * PAGE_SIZE), lambda b_, s, pt: (b_, 0, s)),
    )
    return pl.pallas_call(
        functools.partial(_sample_keys_body, ppstep=ppstep),
        grid_spec=grid_spec,
        out_shape=jax.ShapeDtypeStruct((b, ht, n_pages * PAGE_SIZE), F32),
        compiler_params=_cparams(("arbitrary", "arbitrary"), LARGE_VMEM_LIMIT),
        name="moba_sample_keys",
    )(pt_flat, q3, *([ckT] * ppstep))


def _sample_values_body(pt_ref, sc_ref, q_ref, kn_ref, vn_ref, *refs, ppstep, n_pages, t_new):
    pages, o_ref, (lg_s, p_s, l_s, acc_s) = refs[:ppstep], refs[ppstep], refs[ppstep + 1:]
    s = pl.program_id(1)
    ppb = MOBA_BLOCK // PAGE_SIZE
    n_fp = n_pages // ppb
    n_sel = min(MOBA_TOPK, n_fp)
    past = n_pages * PAGE_SIZE
    ht = MOBA_HEADS * t_new
    bps = ppstep // ppb

    @pl.when(s == 0)
    def _():
        qbd = _head_block_diag(q_ref[0])
        blane = lax.broadcasted_iota(jnp.int32, (ht, LANES), 1)
        gate = jnp.full((ht, LANES), -jnp.inf, F32)
        for n in range(n_fp):
            gsum = jnp.sum(sc_ref[0, :, n * MOBA_BLOCK:(n + 1) * MOBA_BLOCK], axis=1, keepdims=True)
            gate = jnp.where(blane == n, gsum, gate)
        sel = _topk_mask(gate, n_sel, 1)

        row = lax.broadcasted_iota(jnp.int32, (ht, 1), 0)
        slope = jnp.exp2(-((row // t_new) + 1).astype(F32))
        t_col = row % t_new
        qs = (qbd * SCALE).astype(BF16)
        pad = jnp.zeros((LANES - t_new, MOBA_W), F32)
        knew = jnp.concatenate([kn_ref[0], pad], axis=0).astype(BF16)
        vnew = jnp.concatenate([vn_ref[0], pad], axis=0).astype(BF16)
        l_own = _dot_nt(qs, knew) - slope * (t_col - blane).astype(F32)
        l_own = jnp.where(blane <= t_col, l_own, NEG)
        pos = lax.broadcasted_iota(jnp.int32, (ht, MOBA_BLOCK), 1)
        base = slope * (t_col - pos).astype(F32)
        m_el = jnp.full((ht, LANES), NEG, F32)
        for n in range(n_fp):
            cols = slice(n * MOBA_BLOCK, (n + 1) * MOBA_BLOCK)
            ln = sc_ref[0, :, cols] - base - slope * float(past - n * MOBA_BLOCK)
            ln = jnp.where(sel[:, n:n + 1] > 0.5, ln, NEG)
            m_el = jnp.maximum(m_el, jnp.maximum(ln[:, :LANES], ln[:, LANES:]))
            lg_s[:, cols] = ln
        m = jnp.maximum(jnp.max(l_own, axis=1, keepdims=True), jnp.max(m_el, axis=1, keepdims=True))
        p_own = jnp.exp(l_own - m)
        s_el = jnp.zeros((ht, LANES), F32)
        for n in range(n_fp):
            pn = jnp.exp(lg_s[:, n * MOBA_BLOCK:(n + 1) * MOBA_BLOCK] - m)
            s_el = s_el + (pn[:, :LANES] + pn[:, LANES:])
            p_s[n // bps, :, (n % bps) * MOBA_BLOCK:(n % bps + 1) * MOBA_BLOCK] = pn.astype(BF16)
        lsum = jnp.sum(p_own, axis=1, keepdims=True) + jnp.sum(s_el, axis=1, keepdims=True)
        l_s[...] = jnp.broadcast_to(lsum, (ht, LANES))
        acc_s[...] = _dot(p_own.astype(BF16), vnew)

    vt = jnp.concatenate([pages[i][0].reshape(MOBA_W, PAGE_SIZE).astype(BF16) for i in range(ppstep)], axis=1)
    acc_s[...] += _dot_nt(p_s[s], vt)

    @pl.when(s == pl.num_programs(1) - 1)
    def _():
        o_all = acc_s[...] / l_s[:, 0:1]
        lane_head = lax.broadcasted_iota(jnp.int32, (t_new, MOBA_W), 1) // HEAD_DIM
        out = jnp.zeros((t_new, MOBA_W), F32)
        for h in range(MOBA_HEADS):
            out = out + jnp.where(lane_head == h, o_all[h * t_new:(h + 1) * t_new], 0.0)
        o_ref[0] = out


def _sample_values(pt_flat, scores, q3, kn3, vn3, cvT, n_pages):
    b, t, w = q3.shape
    ppstep = min(SAMPLE_PAGES_PER_STEP, n_pages)
    ht = MOBA_HEADS * t
    n_steps = n_pages // ppstep
    small = lambda a: pl.BlockSpec((1,) + a.shape[1:], lambda b_, s, pt: (b_,) + (0,) * (a.ndim - 1))
    grid_spec = pltpu.PrefetchScalarGridSpec(
        num_scalar_prefetch=1,
        grid=(b, n_steps),
        in_specs=[small(scores), small(q3), small(kn3), small(vn3)]
        + [_page_spec(n_pages, ppstep, i) for i in range(ppstep)],
        out_specs=pl.BlockSpec((1, t, w), lambda b_, s, pt: (b_, 0, 0)),
        scratch_shapes=[pltpu.VMEM((ht, n_pages * PAGE_SIZE), F32),
                        pltpu.VMEM((n_steps, ht, ppstep * PAGE_SIZE), BF16),
                        pltpu.VMEM((ht, LANES), F32),
                        pltpu.VMEM((ht, w), F32)],
    )
    return pl.pallas_call(
        functools.partial(_sample_values_body, ppstep=ppstep, n_pages=n_pages, t_new=t),
        grid_spec=grid_spec,
        out_shape=jax.ShapeDtypeStruct((b, t, w), F32),
        compiler_params=_cparams(("arbitrary", "arbitrary"), LARGE_VMEM_LIMIT),
        name="moba_sample_values",
    )(pt_flat, scores, q3, kn3, vn3, *([cvT] * ppstep))


def _sgu_block_diag(w_s, t_new):
    g = w_s.shape[0]
    reps = SGU_CHUNK // t_new
    small = jnp.tril(w_s)[:, :t_new, :t_new]
    eye = jnp.eye(reps, dtype=w_s.dtype)
    return jnp.einsum("rq,gts->grtqs", eye, small).reshape(g, SGU_CHUNK, SGU_CHUNK)


def _sgu_bias(b_s, rows):
    return jnp.repeat(b_s[:, :rows].T, HEAD_DIM, axis=1)


def _ffn(h1, wrT, rb_col, moe_w, g, b):
    gates = _router(h1, wrT, rb_col).T
    return _moe(h1, gates, *moe_w, g, b)


def kernel(x_prompt, x_sample, mem_prompt, cache_k, cache_v, page_table, cache_mem_k, cache_mem_v, w_in, w_out, w_mem_kv, sgu_ln_g, sgu_ln_b, sgu_w_s, sgu_b_s, ln1_g, ln1_b, w_router, router_bias, w_gate, w_up, w_down, ws_gate, ws_up, ws_down, ln2_g, ln2_b):
    assert w_in.shape[0] == DEPTH and x_prompt.shape[0] == 1
    _, seq, d_model = x_prompt.shape
    db, t_new, _ = x_sample.shape
    n_pages = page_table.shape[1]
    assert n_pages % (MOBA_BLOCK // PAGE_SIZE) == 0 and SGU_CHUNK % t_new == 0
    n_s = db * t_new
    assert n_s % SGU_CHUNK == 0 and seq % (8 * MOBA_BLOCK) == 0 and seq % (ATT_GROUP * MOBA_BLOCK) == 0
    assert n_pages % min(SAMPLE_PAGES_PER_STEP, n_pages) == 0

    w_in_bf = w_in[0].astype(BF16)
    wo_bf = w_out[0].astype(BF16)
    wmem_bf = w_mem_kv[0].astype(BF16)
    sg_g, sg_b = sgu_ln_g, sgu_ln_b
    moe_w = (w_gate[0], w_up[0], w_down[0],
             ws_gate[0].astype(BF16), ws_up[0].astype(BF16), ws_down[0].astype(BF16))
    wrT = jnp.swapaxes(w_router[0], 0, 1)
    rb_col = router_bias[0][:, None]

    xp = x_prompt[0]
    qT, k, k_bf, kTp, vTp, vTg, u, svn, qmT = _in_proj_prompt(xp, w_in_bf, sg_g, sg_b)
    mkv, mkvT = _mem_kv(mem_prompt[0], wmem_bf)
    oT = _moba_prompt(qT, k_bf, vTg, _k_block_mean(k))
    ocT = _mem_attend_prompt(qmT, mkv, mkvT)
    wm_p = jnp.tril(sgu_w_s[0]).astype(BF16)
    h1 = _mix(xp, oT, u, svn, ocT, wm_p, _sgu_bias(sgu_b_s[0], SGU_CHUNK), wo_bf, ln1_g, ln1_b, transposed=True)
    y_prompt = _ffn(h1, wrT, rb_col, moe_w, ln2_g, ln2_b)[None]

    n_pg = seq // PAGE_SIZE
    to_pages = lambda zT: jnp.swapaxes(zT.reshape(1, 1, n_pg, MOBA_HEADS, HEAD_DIM, PAGE_SIZE), -1, -2)
    k_prompt, v_prompt = to_pages(kTp), to_pages(vTp)
    n_mem = mkv.shape[0]
    mem_k_prompt = mkv[:, :MEM_W].reshape(1, 1, n_mem, MEM_HEADS, HEAD_DIM)
    mem_v_prompt = mkv[:, MEM_W:].reshape(1, 1, n_mem, MEM_HEADS, HEAD_DIM)

    xs = x_sample.reshape(n_s, d_model)
    q, k, v, u, svn, qm = _in_proj(xs, w_in_bf, sg_g, sg_b)
    heads = lambda z: z.reshape(db, t_new, MOBA_HEADS, HEAD_DIM).transpose(0, 2, 1, 3)
    k4, v4 = heads(k), heads(v)
    ckT = jnp.swapaxes(cache_k[0], -1, -2)
    cvT = jnp.swapaxes(cache_v[0], -1, -2)
    pt_flat = page_table.reshape(-1)
    three = lambda z: z.reshape(db, t_new, MOBA_W)
    scores = _sample_keys(pt_flat, three(q), ckT, n_pages)
    o_a = _sample_values(pt_flat, scores, three(q), three(k), three(v), cvT, n_pages).reshape(n_s, MOBA_W)
    mkT = cache_mem_k[0].transpose(0, 2, 3, 1)
    mvT = cache_mem_v[0].transpose(0, 2, 3, 1)
    o_c = _mem_attend_sample(qm.reshape(db, t_new, MEM_W), mkT, mvT).reshape(n_s, MEM_W)
    wm_s = _sgu_block_diag(sgu_w_s[0], t_new).astype(BF16)
    bz_s = jnp.tile(_sgu_bias(sgu_b_s[0], t_new), (SGU_CHUNK // t_new, 1))
    h1 = _mix(xs, o_a, u, svn, o_c, wm_s, bz_s, wo_bf, ln1_g, ln1_b)
    y_sample = _ffn(h1, wrT, rb_col, moe_w, ln2_g, ln2_b).reshape(db, t_new, d_model)

    k_sample, v_sample = k4[None], v4[None]
    sgu_v_sample = svn.reshape(1, db, t_new, SGU_W)
    return (y_prompt, y_sample, k_prompt, v_prompt, k_sample, v_sample, mem_k_prompt, mem_v_prompt, sgu_v_sample)
```
